```python
import jax, jax.numpy as jnp
from jax import lax
import numpy as np

D_MODEL = 1024
BATCH = 32
SEQ = 256
DEPTH = 4
DEC_BATCH = 2
DEC_SEQ = 1024
PAST_LEN = 256

GRID_W = 64
D_SSM = 1024
HEAD_DIM_SSM = 64
N_HEADS_SSM = D_SSM // HEAD_DIM_SSM
N_GROUPS_SSM = 4
D_STATE = 128
D_CONV = 5
SSD_CHUNK = 128
CONV_DIM = D_SSM + 2 * N_GROUPS_SSM * D_STATE
D_GMLP = 1024
N_GROUPS_GMLP = 8
GMLP_GROUP_DIM = D_GMLP // N_GROUPS_GMLP
GMLP_CHUNK = 128
IN_SIZES = (CONV_DIM, D_SSM, 2 * N_HEADS_SSM, D_GMLP, D_GMLP, D_GMLP, D_MODEL, D_MODEL)
D_IN = CONV_DIM + D_SSM + 2 * N_HEADS_SSM + 3 * D_GMLP + 2 * D_MODEL
EPS = 1e-6

kernel_name = 'hybrid_ssd_gmlp_diffusion_step'


def _split_points(sizes):
    pts, acc = [], 0
    for s in sizes[:-1]:
        acc += s
        pts.append(acc)
    return pts


def rms_norm(x, w):
    xf = x.astype(jnp.float32)
    y = xf * lax.rsqrt(jnp.mean(xf * xf, axis=-1, keepdims=True) + EPS)
    return (y * w.astype(jnp.float32)).astype(x.dtype)


def layer_norm(x, w, b):
    xf = x.astype(jnp.float32)
    mu = jnp.mean(xf, axis=-1, keepdims=True)
    xc = xf - mu
    y = xc * lax.rsqrt(jnp.mean(xc * xc, axis=-1, keepdims=True) + EPS)
    return (y * w.astype(jnp.float32) + b.astype(jnp.float32)).astype(x.dtype)


def _dwconv(x, w, b):
    y = lax.conv_general_dilated(x, w[:, None, :], window_strides=(1,), padding='SAME',
                                 dimension_numbers=('NWC', 'WIO', 'NWC'),
                                 feature_group_count=x.shape[-1])
    return y + b


def _grid_to_col_major(t, rows):
    b, l = t.shape[:2]
    rest = t.shape[2:]
    return t.reshape((b, rows, GRID_W) + rest).swapaxes(1, 2).reshape((b, l) + rest)


def _col_major_to_grid(t, rows):
    b, l = t.shape[:2]
    rest = t.shape[2:]
    return t.reshape((b, GRID_W, rows) + rest).swapaxes(1, 2).reshape((b, l) + rest)


def _segsum(a):
    cs = jnp.cumsum(a, axis=-1)
    diff = cs[..., :, None] - cs[..., None, :]
    t = a.shape[-1]
    mask = jnp.tril(jnp.ones((t, t), dtype=bool))
    return jnp.where(mask, diff, -jnp.inf)


def _ssd_scan(x, dt, a, b_mat, c_mat, h0):
    bsz, l, h, p = x.shape
    n = b_mat.shape[-1]
    rep = h // b_mat.shape[2]
    nc = l // SSD_CHUNK
    bh = jnp.repeat(b_mat, rep, axis=2).reshape(bsz, nc, SSD_CHUNK, h, n)
    chh = jnp.repeat(c_mat, rep, axis=2).reshape(bsz, nc, SSD_CHUNK, h, n)
    xc = (x * dt[..., None]).reshape(bsz, nc, SSD_CHUNK, h, p)
    a_dt = (dt * a).reshape(bsz, nc, SSD_CHUNK, h).transpose(0, 3, 1, 2)
    a_cum = jnp.cumsum(a_dt, axis=-1)
    lmat = jnp.exp(_segsum(a_dt))
    y_diag = jnp.einsum('bclhn,bcshn,bhcls,bcshp->bclhp', chh, bh, lmat, xc)
    decay_states = jnp.exp(a_cum[..., -1:] - a_cum)
    states = jnp.einsum('bcshn,bhcs,bcshp->bchpn', bh, decay_states, xc)
    states = jnp.concatenate([h0[:, None], states], axis=1)
    chunk_decay = jnp.exp(_segsum(jnp.pad(a_cum[..., -1], ((0, 0), (0, 0), (1, 0)))))
    new_states = jnp.einsum('bhzc,bchpn->bzhpn', chunk_decay, states)
    states_in, final = new_states[:, :-1], new_states[:, -1]
    y_off = jnp.einsum('bclhn,bchpn,bhcl->bclhp', chh, states_in, jnp.exp(a_cum))
    return (y_diag + y_off).reshape(bsz, l, h, p), final


def _ssd_branch(xbc_raw, dt_raw, z, h0, conv_w, conv_b, dt_bias, a_log, d_skip, ssm_norm_w, rows):
    col_major = rows is not None
    if col_major:
        xbc_raw = _grid_to_col_major(xbc_raw, rows)
        dt_raw = _grid_to_col_major(dt_raw, rows)
    xbc = jax.nn.silu(_dwconv(xbc_raw, conv_w, conv_b))
    bsz, l, _ = xbc.shape
    xs, bm, cm = jnp.split(xbc, [D_SSM, D_SSM + N_GROUPS_SSM * D_STATE], axis=-1)
    f32 = jnp.float32
    xs = xs.reshape(bsz, l, N_HEADS_SSM, HEAD_DIM_SSM).astype(f32)
    bm = bm.reshape(bsz, l, N_GROUPS_SSM, D_STATE).astype(f32)
    cm = cm.reshape(bsz, l, N_GROUPS_SSM, D_STATE).astype(f32)
    dt = jax.nn.softplus(dt_raw.astype(f32).reshape(bsz, l, 2, N_HEADS_SSM) + dt_bias.astype(f32))
    a = -jnp.exp(a_log.astype(f32))
    h0 = h0.astype(f32)
    y_f, h_f = _ssd_scan(xs, dt[:, :, 0], a[0], bm, cm, h0[:, 0])
    y_b, h_b = _ssd_scan(jnp.flip(xs, 1), jnp.flip(dt[:, :, 1], 1), a[1],
                         jnp.flip(bm, 1), jnp.flip(cm, 1), h0[:, 1])
    y = y_f + jnp.flip(y_b, 1) + d_skip.astype(f32)[:, None] * xs
    y = y.reshape(bsz, l, D_SSM).astype(z.dtype)
    if col_major:
        y = _col_major_to_grid(y, rows)
    y = rms_norm(y * jax.nn.silu(z), ssm_norm_w)
    return y, jnp.stack([h_f, h_b], axis=1)


def _gmlp_branch(u_raw, v_raw, z, ln_w, ln_b, w_sp, b_sp):
    u = jax.nn.gelu(u_raw)
    v = layer_norm(jax.nn.gelu(v_raw), ln_w, ln_b)
    bsz, l, _ = v.shape
    vg = v.reshape(bsz, l // GMLP_CHUNK, GMLP_CHUNK, N_GROUPS_GMLP, GMLP_GROUP_DIM)
    s = jnp.einsum('gts,bnsgc->bntgc', w_sp, vg) + b_sp.T[None, None, :, :, None]
    return u * s.reshape(bsz, l, D_GMLP) * jax.nn.silu(z)


def _layer(x, mod, h0, lw, rows):
    (w_in, norm_w, conv_w, conv_b, dt_bias, a_log, d_skip, ssm_norm_w, w_out_m,
     sgu_ln_w, sgu_ln_b, w_sp, b_sp, w_out_g, w_out) = lw
    shift, scale, gate = jnp.split(mod, 3, axis=-1)
    h = rms_norm(x, norm_w) * (1 + scale[:, None]) + shift[:, None]
    proj = h @ w_in
    xbc_raw, z_m, dt_raw, u_raw, v_raw, z_g, gl_m, gl_g = jnp.split(
        proj, _split_points(IN_SIZES), axis=-1)
    y_m, h_fin = _ssd_branch(xbc_raw, dt_raw, z_m, h0, conv_w, conv_b, dt_bias, a_log,
                             d_skip, ssm_norm_w, rows)
    y_g = _gmlp_branch(u_raw, v_raw, z_g, sgu_ln_w, sgu_ln_b, w_sp, b_sp)
    merged = jax.nn.sigmoid(gl_m) * (y_m @ w_out_m) + jax.nn.sigmoid(gl_g) * (y_g @ w_out_g)
    return x + gate[:, None] * (merged @ w_out), h_fin


def setup_inputs(seed: int = 0) -> dict:
    key = jax.random.key(seed)
    ks = jax.random.split(key, 24)
    nrm = jax.random.normal
    f32 = jnp.float32
    dt_lo = CONV_DIM + D_SSM
    dt_hi = dt_lo + 2 * N_HEADS_SSM
    w_in = nrm(ks[5], (DEPTH, D_MODEL, D_IN), f32) * D_MODEL ** -0.5
    w_in = w_in.at[:, :, dt_lo:dt_hi].multiply(0.1)
    dt_init = jnp.exp(jax.random.uniform(ks[8], (DEPTH, 2, N_HEADS_SSM), f32,
                                         np.log(1e-3), np.log(1e-1)))
    dt_bias = dt_init + jnp.log(-jnp.expm1(-dt_init))
    return {
        'x_prompt': nrm(ks[0], (BATCH, SEQ, D_MODEL), f32),
        'x_sample': nrm(ks[1], (DEC_BATCH, DEC_SEQ, D_MODEL), f32),
        'state_ssm': 0.5 * nrm(ks[2], (DEC_BATCH, DEPTH, 2, N_HEADS_SSM, HEAD_DIM_SSM, D_STATE), f32),
        'c': nrm(ks[3], (DEC_BATCH, D_MODEL), f32),
        'c_ctx': nrm(ks[4], (D_MODEL,), f32),
        'w_ada': nrm(ks[6], (DEPTH, D_MODEL, 3 * D_MODEL), f32) * 0.5 * D_MODEL ** -0.5,
        'b_ada': 0.01 * nrm(ks[7], (DEPTH, 3 * D_MODEL), f32),
        'norm_w': 1.0 + 0.1 * nrm(ks[9], (DEPTH, D_MODEL), f32),
        'w_in': w_in,
        'conv_w': nrm(ks[10], (DEPTH, D_CONV, CONV_DIM), f32) * D_CONV ** -0.5,
        'conv_b': 0.01 * nrm(ks[11], (DEPTH, CONV_DIM), f32),
        'dt_bias': dt_bias,
        'a_log': jnp.log(jax.random.uniform(ks[12], (DEPTH, 2, N_HEADS_SSM), f32, 1.0, 16.0)),
        'd_skip': 1.0 + 0.1 * nrm(ks[13], (DEPTH, N_HEADS_SSM), f32),
        'ssm_norm_w': 1.0 + 0.1 * nrm(ks[14], (DEPTH, D_SSM), f32),
        'w_out_m': nrm(ks[15], (DEPTH, D_SSM, D_MODEL), f32) * D_SSM ** -0.5,
        'sgu_ln_w': 1.0 + 0.1 * nrm(ks[16], (DEPTH, D_GMLP), f32),
        'sgu_ln_b': 0.1 * nrm(ks[17], (DEPTH, D_GMLP), f32),
        'w_sp': nrm(ks[18], (DEPTH, N_GROUPS_GMLP, GMLP_CHUNK, GMLP_CHUNK), f32) * GMLP_CHUNK ** -0.5,
        'b_sp': 1.0 + 0.1 * nrm(ks[19], (DEPTH, N_GROUPS_GMLP, GMLP_CHUNK), f32),
        'w_out_g': nrm(ks[20], (DEPTH, D_GMLP, D_MODEL), f32) * D_GMLP ** -0.5,
        'w_out': nrm(ks[21], (DEPTH, D_MODEL, D_MODEL), f32) * D_MODEL ** -0.5,
        'final_norm_w': 1.0 + 0.1 * nrm(ks[22], (D_MODEL,), f32),
    }


def reference(x_prompt, x_sample, state_ssm, c, c_ctx, w_ada, b_ada, norm_w, w_in, conv_w,
              conv_b, dt_bias, a_log, d_skip, ssm_norm_w, w_out_m, sgu_ln_w, sgu_ln_b,
              w_sp, b_sp, w_out_g, w_out, final_norm_w):
    rows = x_sample.shape[1] // GRID_W
    h_ctx = x_prompt
    h_lat = x_sample
    h0_ctx = jnp.zeros((x_prompt.shape[0], 2, N_HEADS_SSM, HEAD_DIM_SSM, D_STATE), jnp.float32)
    silu_c = jax.nn.silu(c)
    silu_cctx = jax.nn.silu(c_ctx)[None, :]
    ctx_states = []
    for i in range(DEPTH):
        lw = (w_in[i], norm_w[i], conv_w[i], conv_b[i], dt_bias[i], a_log[i], d_skip[i],
              ssm_norm_w[i], w_out_m[i], sgu_ln_w[i], sgu_ln_b[i], w_sp[i], b_sp[i],
              w_out_g[i], w_out[i])
        mod_ctx = silu_cctx @ w_ada[i] + b_ada[i]
        mod_lat = silu_c @ w_ada[i] + b_ada[i]
        h_ctx, st = _layer(h_ctx, mod_ctx, h0_ctx, lw, None)
        ctx_states.append(st.astype(x_prompt.dtype))
        h_lat, _ = _layer(h_lat, mod_lat, state_ssm[:, i], lw, rows if i % 2 == 1 else None)
    new_state_ssm = jnp.stack(ctx_states, axis=1)
    y_prompt = rms_norm(h_ctx, final_norm_w)
    y_sample = rms_norm(h_lat, final_norm_w)
    return (y_prompt, y_sample, new_state_ssm)
```

```python
import functools

import jax
import jax.numpy as jnp
import numpy as np
from jax import lax
from jax.experimental import pallas as pl
from jax.experimental.pallas import tpu as pltpu

F32 = jnp.float32
BF16 = jnp.bfloat16

D_MODEL = 1024
DEPTH = 4
GRID_W = 64
D_SSM = 1024
HEAD_DIM = 64
N_HEADS = 16
N_GROUPS = 4
HEADS_PER_GROUP = N_HEADS // N_GROUPS
D_STATE = 128
D_CONV = 5
CHUNK = 128
CONV_DIM = D_SSM + 2 * N_GROUPS * D_STATE
D_GMLP = 1024
N_GROUPS_GMLP = 8
IN_SIZES = (CONV_DIM, D_SSM, 2 * N_HEADS, D_GMLP, D_GMLP, D_GMLP, D_MODEL, D_MODEL)
EPS = 1e-6

LANES = 128
SUBLANES = 8
CONV_PAD = SUBLANES
CONV_TILE = 512
GROUP_W = HEADS_PER_GROUP * HEAD_DIM
ROW_TILE = 256
VMEM_LIMIT_BYTES = 60 * 1024 * 1024


def _dot(a, b):
    return jnp.dot(a, b, preferred_element_type=F32)


def _split_bf16(a, parts):
    out = []
    rem = a
    for i in range(parts):
        p = rem.astype(BF16)
        out.append(p)
        if i + 1 < parts:
            rem = rem - p.astype(F32)
    return out


def _dot_exact_rhs(a, m_bf16, parts):
    acc = None
    for p in _split_bf16(a, parts):
        t = _dot(p, m_bf16)
        acc = t if acc is None else acc + t
    return acc


def _dot_exact_lhs(m_bf16, a, parts):
    acc = None
    for p in _split_bf16(a, parts):
        t = _dot(m_bf16, p)
        acc = t if acc is None else acc + t
    return acc


def _sigmoid(x):
    return 1.0 / (1.0 + jnp.exp(-x))


def _silu(x):
    return x * _sigmoid(x)


def _gelu_tanh(x):
    c = np.float32(np.sqrt(2.0 / np.pi))
    return x * (0.5 * (1.0 + jnp.tanh(c * (x + 0.044715 * (x * x * x)))))


def _softplus(x):
    return jnp.maximum(x, 0.0) + jnp.log1p(jnp.exp(-jnp.abs(x)))


def _mean_last(x):
    return jnp.mean(x, axis=-1, keepdims=True)


def _mod_kernel(c_ref, w_ref, b_ref, o_ref):
    sc = _silu(c_ref[...]).astype(BF16)
    o_ref[0] = _dot(sc, w_ref[0].astype(BF16)) + b_ref[0]


def _modulation(cc, w_ada, b_ada):
    rows = cc.shape[0]
    ncol = 3 * D_MODEL // D_MODEL
    return pl.pallas_call(
        _mod_kernel,
        grid=(DEPTH, ncol),
        in_specs=[
            pl.BlockSpec((rows, D_MODEL), lambda i, j: (0, 0)),
            pl.BlockSpec((1, D_MODEL, D_MODEL), lambda i, j: (i, 0, j)),
            pl.BlockSpec((1, 1, D_MODEL), lambda i, j: (i, 0, j)),
        ],
        out_specs=pl.BlockSpec((1, rows, D_MODEL), lambda i, j: (i, 0, j)),
        out_shape=jax.ShapeDtypeStruct((DEPTH, rows, 3 * D_MODEL), F32),
        name="adaln_mod",
    )(cc, w_ada, b_ada.reshape(DEPTH, 1, 3 * D_MODEL))


def _layer_kernel(*refs, seq_len, col_major, has_h0, want_state, final_norm):
    L = seq_len
    nc = L // CHUNK
    it = iter(refs)
    x_ref = next(it)
    mod_ref = next(it)
    h0_ref = next(it) if has_h0 else None
    (wxbc_ref, wzm_ref, wdt_ref, wu_ref, wv_ref, wzg_ref, wglm_ref, wglg_ref,
     woutm_ref, woutg_ref, wout_ref, wsp_ref,
     normw_ref, convw_ref, convb_ref, dtb_ref, alog_ref, dskip_ref, ssmw_ref,
     lnw_ref, lnb_ref, bsp_ref, finw_ref, expand_ref) = (next(it) for _ in range(24))
    out_ref = next(it)
    st_ref = next(it) if want_state else None
    (h_scr, hcm_scr, xraw_scr, xs_scr, c_scr, bt_scr, g_scr, y_scr, st_scr,
     dt_scr) = (next(it) for _ in range(10))

    shift = mod_ref[0, 0:1, :]
    scale = mod_ref[0, 1:2, :]
    gate = mod_ref[0, 2:3, :]
    normw = normw_ref[...]

    def norm_mod(xt):
        y = xt * lax.rsqrt(_mean_last(xt * xt) + EPS) * normw
        return (y * (1.0 + scale) + shift).astype(BF16)

    for r0 in range(0, L, ROW_TILE):
        h_scr[r0:r0 + ROW_TILE, :] = norm_mod(x_ref[0, r0:r0 + ROW_TILE, :])
    if col_major:
        rows_per_col = L // GRID_W
        for r0 in range(0, L, ROW_TILE):
            dst = r0 + lax.broadcasted_iota(jnp.int32, (ROW_TILE, L), 0)
            src = lax.broadcasted_iota(jnp.int32, (ROW_TILE, L), 1)
            want = (dst % rows_per_col) * GRID_W + dst // rows_per_col
            perm = jnp.where(src == want, 1.0, 0.0).astype(BF16)
            hcm_scr[r0:r0 + ROW_TILE, :] = _dot(perm, h_scr[...]).astype(BF16)
        hsrc = hcm_scr
    else:
        hsrc = h_scr

    zero_pad = jnp.zeros((CONV_PAD, CONV_TILE), F32)
    xraw_scr[0:CONV_PAD, :] = zero_pad
    xraw_scr[CONV_PAD + L:CONV_PAD + L + CONV_PAD, :] = zero_pad
    half = D_CONV // 2
    for j in range(CONV_DIM // CONV_TILE):
        cols = slice(j * CONV_TILE, (j + 1) * CONV_TILE)
        xraw_scr[CONV_PAD:CONV_PAD + L, :] = _dot(hsrc[...], wxbc_ref[:, cols])
        for r0 in range(0, L, ROW_TILE):
            acc = convb_ref[:, cols]
            for k in range(D_CONV):
                start = CONV_PAD + r0 + k - half
                acc = acc + convw_ref[k:k + 1, cols] * xraw_scr[start:start + ROW_TILE, :]
            act = _silu(acc)
            if j * CONV_TILE < D_SSM:
                xs_scr[r0:r0 + ROW_TILE, cols] = act
            elif j * CONV_TILE < D_SSM + N_GROUPS * D_STATE:
                for cc in range(ROW_TILE // CHUNK):
                    for g in range(N_GROUPS):
                        blk = act[cc * CHUNK:(cc + 1) * CHUNK, g * D_STATE:(g + 1) * D_STATE]
                        bt_scr[r0 // CHUNK + cc, g * D_STATE:(g + 1) * D_STATE, :] = (
                            blk.T.astype(BF16))
            else:
                c_scr[r0:r0 + ROW_TILE, :] = act.astype(BF16)

    lane = lax.broadcasted_iota(jnp.int32, (1, LANES), 1)
    for d in range(2):
        raw = _dot(hsrc[...], wdt_ref[d]) + dtb_ref[d:d + 1, :]
        dt_scr[d] = _softplus(raw)
    a_rows = [jnp.where(lane < N_HEADS, -jnp.exp(alog_ref[d:d + 1, :]), 0.0) for d in range(2)]

    row_i = lax.broadcasted_iota(jnp.int32, (CHUNK, CHUNK), 0)
    col_i = lax.broadcasted_iota(jnp.int32, (CHUNK, CHUNK), 1)
    causal = [row_i >= col_i, row_i <= col_i]
    tri = [jnp.where(m, 1.0, 0.0).astype(BF16) for m in causal]
    glane = lax.broadcasted_iota(jnp.int32, (CHUNK, GROUP_W), 1)
    head_lanes = [(glane >= hh * HEAD_DIM) & (glane < (hh + 1) * HEAD_DIM)
                  for hh in range(HEADS_PER_GROUP)]
    expand = expand_ref[...]
    dskip = dskip_ref[...]

    def ssd_chunk(i, carry, d):
        c = i if d == 0 else nc - 1 - i
        rows = pl.ds(pl.multiple_of(c * CHUNK, CHUNK), CHUNK)
        edge = CHUNK - 1 if d == 0 else 0
        dt_c = dt_scr[d, rows, :]
        cum = _dot_exact_lhs(tri[d], dt_c * a_rows[d], 3)
        cum_t = cum.T
        ecum = jnp.exp(cum)
        decay = jnp.exp(cum[edge:edge + 1, :] - cum)
        dt_e = _dot_exact_rhs(dt_c, expand, 2)
        ecum_e = _dot_exact_rhs(ecum, expand, 2)
        decay_e = _dot_exact_rhs(decay, expand, 2)
        xs_c = xs_scr[rows, :]
        xdt = xs_c * dt_e
        xdt_b = xdt.astype(BF16)
        xdd_b = (xdt * decay_e).astype(BF16)
        st_decay = ecum_e[edge:edge + 1, :]
        for g in range(N_GROUPS):
            cs = slice(g * GROUP_W, (g + 1) * GROUP_W)
            ns = slice(g * D_STATE, (g + 1) * D_STATE)
            c_g = c_scr[rows, ns]
            bt_g = bt_scr[c, ns, :]
            if d == 0:
                cb = _dot(c_g, bt_g)
                g_scr[c, ns, :] = cb
            else:
                cb = g_scr[c, ns, :]
            st_g = st_scr[:, cs]
            acc = _dot(c_g, st_g.astype(BF16)) * ecum_e[:, cs]
            x_g = xdt_b[:, cs]
            for hh in range(HEADS_PER_GROUP):
                h = g * HEADS_PER_GROUP + hh
                seg = cum[:, h:h + 1] - cum_t[h:h + 1, :]
                lmat = jnp.exp(jnp.where(causal[d], seg, -jnp.inf))
                m = (cb * lmat).astype(BF16)
                x_h = jnp.where(head_lanes[hh], x_g, jnp.zeros_like(x_g))
                acc = acc + _dot(m, x_h)
            if d == 0:
                acc = acc + dskip[:, cs] * xs_c[:, cs]
            for k in range(GROUP_W // LANES):
                kt = g * (GROUP_W // LANES) + k
                part = acc[:, k * LANES:(k + 1) * LANES]
                y_scr[kt, rows, :] = part if d == 0 else y_scr[kt, rows, :] + part
            st_scr[:, cs] = st_g * st_decay[:, cs] + _dot(bt_g, xdd_b[:, cs])
        return carry

    for d in range(2):
        for k in range(D_SSM // D_STATE):
            ks = slice(k * D_STATE, (k + 1) * D_STATE)
            if has_h0:
                st_scr[:, ks] = h0_ref[0, d, ks, :].T
            else:
                st_scr[:, ks] = jnp.zeros((D_STATE, D_STATE), F32)
        lax.fori_loop(0, nc, functools.partial(ssd_chunk, d=d), 0)
        if want_state:
            for k in range(D_SSM // D_STATE):
                ks = slice(k * D_STATE, (k + 1) * D_STATE)
                st_ref[0, d, ks, :] = st_scr[:, ks].T

    n_lane_tiles = D_SSM // LANES
    if col_major:
        rows_per_col = L // GRID_W
        for r in range(rows_per_col):
            for kt in range(n_lane_tiles):
                xs_scr[r * GRID_W:(r + 1) * GRID_W, kt * LANES:(kt + 1) * LANES] = (
                    y_scr[kt, pl.ds(r, GRID_W, stride=rows_per_col), :])

    def tail(t, carry):
        rows = pl.ds(pl.multiple_of(t * ROW_TILE, ROW_TILE), ROW_TILE)
        hn = h_scr[rows, :]
        if col_major:
            y_t = xs_scr[rows, :]
        else:
            y_t = jnp.concatenate([y_scr[kt, rows, :] for kt in range(n_lane_tiles)], axis=1)
        yz = y_t * _silu(_dot(hn, wzm_ref[...]))
        ym = (yz * lax.rsqrt(_mean_last(yz * yz) + EPS) * ssmw_ref[...]).astype(BF16)
        merged = _sigmoid(_dot(hn, wglm_ref[...])) * _dot(ym, woutm_ref[...])

        v = _gelu_tanh(_dot(hn, wv_ref[...]))
        vc = v - _mean_last(v)
        vn = vc * lax.rsqrt(_mean_last(vc * vc) + EPS) * lnw_ref[...] + lnb_ref[...]
        vb = vn.astype(BF16)
        s_rows = []
        for cc in range(ROW_TILE // CHUNK):
            parts = [_dot(wsp_ref[g], vb[cc * CHUNK:(cc + 1) * CHUNK, g * CHUNK:(g + 1) * CHUNK])
                     for g in range(N_GROUPS_GMLP)]
            s_rows.append(jnp.concatenate(parts, axis=1) + bsp_ref[...])
        s = jnp.concatenate(s_rows, axis=0)
        u = _gelu_tanh(_dot(hn, wu_ref[...]))
        yg = (u * s * _silu(_dot(hn, wzg_ref[...]))).astype(BF16)
        merged = merged + _sigmoid(_dot(hn, wglg_ref[...])) * _dot(yg, woutg_ref[...])

        o = x_ref[0, rows, :] + gate * _dot(merged.astype(BF16), wout_ref[...])
        if final_norm:
            o = o * lax.rsqrt(_mean_last(o * o) + EPS) * finw_ref[...]
        out_ref[0, rows, :] = o
        return carry

    lax.fori_loop(0, L // ROW_TILE, tail, 0)


def _const_spec(shape):
    nd = len(shape)
    return pl.BlockSpec(shape, lambda b, _nd=nd: (0,) * _nd, pipeline_mode=pl.Buffered(1))


def _layer_call(x, mod, h0, weights, *, col_major, want_state, final_norm, mod_per_seq):
    nb, L, _ = x.shape
    nc = L // CHUNK
    has_h0 = h0 is not None
    kernel = functools.partial(_layer_kernel, seq_len=L, col_major=col_major, has_h0=has_h0,
                               want_state=want_state, final_norm=final_norm)
    single = pl.Buffered(1) if L * D_MODEL * 4 >= (4 << 20) else None
    in_specs = [pl.BlockSpec((1, L, D_MODEL), lambda b: (b, 0, 0), pipeline_mode=single),
                pl.BlockSpec((1, 3, D_MODEL), (lambda b: (b, 0, 0)) if mod_per_seq
                             else (lambda b: (0, 0, 0)))]
    args = [x, mod]
    if has_h0:
        in_specs.append(pl.BlockSpec((1, 2, D_SSM, D_STATE), lambda b: (b, 0, 0, 0)))
        args.append(h0)
    for w in weights:
        in_specs.append(_const_spec(w.shape))
        args.append(w)
    out_shape = [jax.ShapeDtypeStruct((nb, L, D_MODEL), F32)]
    out_specs = [pl.BlockSpec((1, L, D_MODEL), lambda b: (b, 0, 0), pipeline_mode=single)]
    if want_state:
        out_shape.append(jax.ShapeDtypeStruct((nb, 2, D_SSM, D_STATE), F32))
        out_specs.append(pl.BlockSpec((1, 2, D_SSM, D_STATE), lambda b: (b, 0, 0, 0)))
    scratch = [
        pltpu.VMEM((L, D_MODEL), BF16),
        pltpu.VMEM((L if col_major else SUBLANES * 2, D_MODEL), BF16),
        pltpu.VMEM((L + 2 * CONV_PAD, CONV_TILE), F32),
        pltpu.VMEM((L, D_SSM), F32),
        pltpu.VMEM((L, N_GROUPS * D_STATE), BF16),
        pltpu.VMEM((nc, N_GROUPS * D_STATE, CHUNK), BF16),
        pltpu.VMEM((nc, N_GROUPS * D_STATE, CHUNK), F32),
        pltpu.VMEM((D_SSM // LANES, L, LANES), F32),
        pltpu.VMEM((D_STATE, D_SSM), F32),
        pltpu.VMEM((2, L, LANES), F32),
    ]
    res = pl.pallas_call(
        kernel,
        grid=(nb,),
        in_specs=in_specs,
        out_specs=out_specs,
        out_shape=out_shape,
        scratch_shapes=scratch,
        compiler_params=pltpu.CompilerParams(
            dimension_semantics=("arbitrary",), vmem_limit_bytes=VMEM_LIMIT_BYTES),
        name="layer_L%d%s" % (L, "_cm" if col_major else ""),
    )(*args)
    return res if want_state else (res[0], None)


def _prep_layer_weights(i, w_in, norm_w, conv_w, conv_b, dt_bias, a_log, d_skip, ssm_norm_w,
                        w_out_m, sgu_ln_w, sgu_ln_b, w_sp, b_sp, w_out_g, w_out, final_norm_w,
                        expand):
    pts = np.cumsum((0,) + IN_SIZES)
    seg = [w_in[i][:, pts[k]:pts[k + 1]].astype(BF16) for k in range(len(IN_SIZES))]
    w_xbc, w_zm, w_dt, w_u, w_v, w_zg, w_glm, w_glg = seg
    w_dt = w_dt.reshape(D_MODEL, 2, N_HEADS).transpose(1, 0, 2)
    w_dt = jnp.pad(w_dt, ((0, 0), (0, 0), (0, LANES - N_HEADS)))
    pad_h = ((0, 0), (0, LANES - N_HEADS))
    row = lambda v: v.reshape(1, -1)
    return [
        w_xbc, w_zm, w_dt, w_u, w_v, w_zg, w_glm, w_glg,
        w_out_m[i].astype(BF16), w_out_g[i].astype(BF16), w_out[i].astype(BF16),
        w_sp[i].astype(BF16),
        row(norm_w[i]), conv_w[i], row(conv_b[i]),
        jnp.pad(dt_bias[i], pad_h), jnp.pad(a_log[i], pad_h),
        row(jnp.repeat(d_skip[i], HEAD_DIM)), row(ssm_norm_w[i]),
        row(sgu_ln_w[i]), row(sgu_ln_b[i]),
        jnp.repeat(b_sp[i].T, D_GMLP // N_GROUPS_GMLP, axis=1),
        row(final_norm_w), expand,
    ]


def kernel(x_prompt, x_sample, state_ssm, c, c_ctx, w_ada, b_ada, norm_w, w_in, conv_w, conv_b,
           dt_bias, a_log, d_skip, ssm_norm_w, w_out_m, sgu_ln_w, sgu_ln_b, w_sp, b_sp, w_out_g,
           w_out, final_norm_w):
    n_lat = c.shape[0]
    n_ctx = x_prompt.shape[0]
    cc = jnp.concatenate([c_ctx[None, :], c,
                          jnp.zeros((SUBLANES - 1 - n_lat, D_MODEL), F32)], axis=0)
    mod = _modulation(cc, w_ada, b_ada).reshape(DEPTH, SUBLANES, 3, D_MODEL)
    expand = (jnp.arange(LANES)[:, None] == (jnp.arange(D_SSM)[None, :] // HEAD_DIM)).astype(BF16)
    h0_lat = state_ssm.reshape(n_lat, DEPTH, 2, D_SSM, D_STATE)

    h_ctx, h_lat = x_prompt, x_sample
    states = []
    for i in range(DEPTH):
        weights = _prep_layer_weights(i, w_in, norm_w, conv_w, conv_b, dt_bias, a_log, d_skip,
                                      ssm_norm_w, w_out_m, sgu_ln_w, sgu_ln_b, w_sp, b_sp,
                                      w_out_g, w_out, final_norm_w, expand)
        last = i == DEPTH - 1
        h_ctx, st = _layer_call(h_ctx, mod[i, 0:1], None, weights, col_major=False,
                                want_state=True, final_norm=last, mod_per_seq=False)
        states.append(st.reshape(n_ctx, 2, N_HEADS, HEAD_DIM, D_STATE))
        h_lat, _ = _layer_call(h_lat, mod[i, 1:1 + n_lat], h0_lat[:, i], weights,
                               col_major=(i % 2 == 1), want_state=False, final_norm=last,
                               mod_per_seq=True)
    return h_ctx, h_lat, jnp.stack(states, axis=1)
```

```python
import functools

import jax
import jax.numpy as jnp
import numpy as np
from jax import lax
from jax.experimental import pallas as pl
from jax.experimental.pallas import tpu as pltpu

F32 = jnp.float32
BF16 = jnp.bfloat16

D_MODEL = 1024
DEPTH = 4
GRID_W = 64
D_SSM = 1024
HEAD_DIM = 64
N_HEADS = 16
N_GROUPS = 4
HEADS_PER_GROUP = N_HEADS // N_GROUPS
D_STATE = 128
D_CONV = 5
CHUNK = 128
CONV_DIM = D_SSM + 2 * N_GROUPS * D_STATE
D_GMLP = 1024
N_GROUPS_GMLP = 8
IN_SIZES = (CONV_DIM, D_SSM, 2 * N_HEADS, D_GMLP, D_GMLP, D_GMLP, D_MODEL, D_MODEL)
EPS = 1e-6

LANES = 128
SUBLANES = 8
CONV_PAD = SUBLANES
CONV_TILE = 256
GROUP_W = HEADS_PER_GROUP * HEAD_DIM
ROW_TILE = 256
VMEM_LIMIT_BYTES = 60 * 1024 * 1024
LOG2_E = np.float32(1.4426950408889634)


def _dot(a, b):
    return jnp.dot(a, b, preferred_element_type=F32)


def _split_bf16(a, parts):
    out = []
    rem = a
    for i in range(parts):
        p = rem.astype(BF16)
        out.append(p)
        if i + 1 < parts:
            rem = rem - p.astype(F32)
    return out


def _dot_exact_rhs(a, m_bf16, parts):
    acc = None
    for p in _split_bf16(a, parts):
        t = _dot(p, m_bf16)
        acc = t if acc is None else acc + t
    return acc


def _dot_exact_lhs(m_bf16, a, parts):
    acc = None
    for p in _split_bf16(a, parts):
        t = _dot(m_bf16, p)
        acc = t if acc is None else acc + t
    return acc


def _sigmoid(x):
    return 1.0 / (1.0 + jnp.exp(-x))


def _silu(x):
    return x * _sigmoid(x)


def _gelu_tanh(x):
    c = np.float32(np.sqrt(2.0 / np.pi))
    return x * (0.5 * (1.0 + jnp.tanh(c * (x + 0.044715 * (x * x * x)))))


def _softplus(x):
    return jnp.maximum(x, 0.0) + jnp.log1p(jnp.exp(-jnp.abs(x)))


def _mean_last(x):
    return jnp.mean(x, axis=-1, keepdims=True)


def _mod_kernel(c_ref, w_ref, b_ref, o_ref):
    sc = _silu(c_ref[...]).astype(BF16)
    o_ref[0] = _dot(sc, w_ref[0].astype(BF16)) + b_ref[0]


def _modulation(cc, w_ada, b_ada):
    rows = cc.shape[0]
    ncol = 3 * D_MODEL // D_MODEL
    return pl.pallas_call(
        _mod_kernel,
        grid=(DEPTH, ncol),
        in_specs=[
            pl.BlockSpec((rows, D_MODEL), lambda i, j: (0, 0)),
            pl.BlockSpec((1, D_MODEL, D_MODEL), lambda i, j: (i, 0, j)),
            pl.BlockSpec((1, 1, D_MODEL), lambda i, j: (i, 0, j)),
        ],
        out_specs=pl.BlockSpec((1, rows, D_MODEL), lambda i, j: (i, 0, j)),
        out_shape=jax.ShapeDtypeStruct((DEPTH, rows, 3 * D_MODEL), F32),
        name="adaln_mod",
    )(cc, w_ada, b_ada.reshape(DEPTH, 1, 3 * D_MODEL))


def _layer_kernel(*refs, seq_len, col_major, has_h0, want_state, final_norm, state_aliased):
    L = seq_len
    nc = L // CHUNK
    it = iter(refs)
    x_ref = next(it)
    mod_ref = next(it)
    h0_ref = next(it) if has_h0 else None
    (wxbc_ref, wzm_ref, wdt_ref, wu_ref, wv_ref, wzg_ref, wglm_ref, wglg_ref,
     woutm_ref, woutg_ref, wout_ref, wsp_ref,
     normw_ref, convw_ref, convb_ref, dtb_ref, alog_ref, dskip_ref, ssmw_ref,
     lnw_ref, lnb_ref, bsp_ref, finw_ref, expand_ref) = (next(it) for _ in range(24))
    if state_aliased:
        next(it)
    out_ref = next(it)
    st_ref = next(it) if want_state else None
    (h_scr, hcm_scr, xraw_scr, xs_scr, c_scr, bt_scr, y_scr, st_scr,
     dt_scr, v_scr, s_scr, cum_scr, srct_scr, wt_scr, sdec_scr) = (next(it) for _ in range(15))

    shift = mod_ref[0, 0:1, :]
    scale = mod_ref[0, 1:2, :]
    gate = mod_ref[0, 2:3, :]
    normw = normw_ref[...]
    n_row_tiles = L // ROW_TILE

    def norm_mod(xt):
        y = xt * lax.rsqrt(_mean_last(xt * xt) + EPS) * normw
        return (y * (1.0 + scale) + shift).astype(BF16)

    def row_tile(t):
        if isinstance(t, int):
            return slice(t * ROW_TILE, (t + 1) * ROW_TILE)
        return pl.ds(pl.multiple_of(t * ROW_TILE, ROW_TILE), ROW_TILE)

    def over_row_tiles(body):
        if n_row_tiles == 1:
            body(0, 0)
        else:
            lax.fori_loop(0, n_row_tiles, body, 0)

    for r0 in range(0, L, ROW_TILE):
        h_scr[r0:r0 + ROW_TILE, :] = norm_mod(x_ref[0, r0:r0 + ROW_TILE, :])
    if col_major:
        rows_per_col = L // GRID_W
        for r0 in range(0, L, ROW_TILE):
            dst = r0 + lax.broadcasted_iota(jnp.int32, (ROW_TILE, L), 0)
            src = lax.broadcasted_iota(jnp.int32, (ROW_TILE, L), 1)
            want = (dst % rows_per_col) * GRID_W + dst // rows_per_col
            perm = jnp.where(src == want, 1.0, 0.0).astype(BF16)
            hcm_scr[r0:r0 + ROW_TILE, :] = _dot(perm, h_scr[...]).astype(BF16)
        hsrc = hcm_scr
    else:
        hsrc = h_scr

    def gmlp_tile(t, carry):
        rows = row_tile(t)
        hn = h_scr[rows, :]
        v = _gelu_tanh(_dot(hn, wv_ref[...]))
        vc = v - _mean_last(v)
        v_scr[...] = (vc * lax.rsqrt(_mean_last(vc * vc) + EPS) * lnw_ref[...]
                      + lnb_ref[...]).astype(BF16)
        for cc in range(ROW_TILE // CHUNK):
            for g in range(N_GROUPS_GMLP):
                blk = (slice(cc * CHUNK, (cc + 1) * CHUNK), slice(g * CHUNK, (g + 1) * CHUNK))
                s_scr[blk] = _dot(wsp_ref[g], v_scr[blk]) + bsp_ref[:, blk[1]]
        u = _gelu_tanh(_dot(hn, wu_ref[...]))
        yg = (u * s_scr[...] * _silu(_dot(hn, wzg_ref[...]))).astype(BF16)
        out_ref[0, rows, :] = _sigmoid(_dot(hn, wglg_ref[...])) * _dot(yg, woutg_ref[...])
        return carry

    over_row_tiles(gmlp_tile)

    zero_pad = jnp.zeros((CONV_PAD, CONV_TILE), F32)
    for buf in range(2):
        xraw_scr[buf, 0:CONV_PAD, :] = zero_pad
        xraw_scr[buf, CONV_PAD + L:CONV_PAD + L + CONV_PAD, :] = zero_pad
    half = D_CONV // 2
    n_conv_tiles = CONV_DIM // CONV_TILE
    b_tile0 = D_SSM // CONV_TILE
    c_tile0 = (D_SSM + N_GROUPS * D_STATE) // CONV_TILE

    def project_conv_tile(j):
        cols = slice(j * CONV_TILE, (j + 1) * CONV_TILE)
        xraw_scr[j % 2, CONV_PAD:CONV_PAD + L, :] = _dot(hsrc[...], wxbc_ref[:, cols])

    project_conv_tile(0)
    for j in range(n_conv_tiles):
        if j + 1 < n_conv_tiles:
            project_conv_tile(j + 1)
        cols = slice(j * CONV_TILE, (j + 1) * CONV_TILE)
        for r0 in range(0, L, ROW_TILE):
            acc = convb_ref[:, cols]
            for k in range(D_CONV):
                start = CONV_PAD + r0 + k - half
                acc = acc + convw_ref[k:k + 1, cols] * xraw_scr[j % 2, start:start + ROW_TILE, :]
            act = _silu(acc)
            if j < b_tile0:
                xs_scr[r0:r0 + ROW_TILE, cols] = act
                skip = dskip_ref[:, cols] * act
                for k in range(CONV_TILE // LANES):
                    y_scr[j * (CONV_TILE // LANES) + k, r0:r0 + ROW_TILE, :] = (
                        skip[:, k * LANES:(k + 1) * LANES])
            elif j < c_tile0:
                bcols = (j - b_tile0) * CONV_TILE
                for cc in range(ROW_TILE // CHUNK):
                    for gg in range(CONV_TILE // D_STATE):
                        blk = act[cc * CHUNK:(cc + 1) * CHUNK, gg * D_STATE:(gg + 1) * D_STATE]
                        bt_scr[r0 // CHUNK + cc,
                               bcols + gg * D_STATE:bcols + (gg + 1) * D_STATE, :] = blk.T
            else:
                ccols = (j - c_tile0) * CONV_TILE
                c_scr[r0:r0 + ROW_TILE, ccols:ccols + CONV_TILE] = act

    lane = lax.broadcasted_iota(jnp.int32, (1, LANES), 1)
    for d in range(2):
        raw = _dot(hsrc[...], wdt_ref[d]) + dtb_ref[d:d + 1, :]
        dt_scr[d] = _softplus(raw)
    a_rows = [jnp.where(lane < N_HEADS, -jnp.exp(alog_ref[d:d + 1, :]) * LOG2_E, 0.0)
              for d in range(2)]

    row_i = lax.broadcasted_iota(jnp.int32, (CHUNK, CHUNK), 0)
    col_i = lax.broadcasted_iota(jnp.int32, (CHUNK, CHUNK), 1)
    causal = [row_i >= col_i, row_i <= col_i]
    tri = [jnp.where(m, 1.0, 0.0).astype(BF16) for m in causal]
    glane = lax.broadcasted_iota(jnp.int32, (CHUNK, GROUP_W), 1)
    head_lanes = [(glane >= hh * HEAD_DIM) & (glane < (hh + 1) * HEAD_DIM)
                  for hh in range(HEADS_PER_GROUP)]
    expand = expand_ref[...]

    tables = []
    for d in range(2):
        for c in range(nc):
            rows = slice(c * CHUNK, (c + 1) * CHUNK)
            dt_c = dt_scr[d, rows, :]
            tables.append((d, c, rows, dt_c, _dot_exact_lhs(tri[d], dt_c * a_rows[d], 3)))
    for d, c, rows, dt_c, cum in tables:
        edge = CHUNK - 1 if d == 0 else 0
        edge_slab = slice(CHUNK - SUBLANES, CHUNK) if d == 0 else slice(0, SUBLANES)
        cum_scr[d, rows, :] = cum
        srct_scr[d, c] = (cum.T - jnp.log2(dt_c.T))[0:N_HEADS, :]
        wt_scr[d, c] = (dt_c * jnp.exp2(cum[edge:edge + 1, :] - cum)).T[0:N_HEADS, :]
        sdec_scr[d, c] = _dot_exact_rhs(jnp.exp2(cum[edge_slab, :]), expand, 2)

    def ssd_chunk(i, d):
        c = i if d == 0 else nc - 1 - i
        rows = pl.ds(pl.multiple_of(c * CHUNK, CHUNK), CHUNK)
        edge = CHUNK - 1 if d == 0 else 0
        cum = cum_scr[d, rows, :]
        src_t = srct_scr[d, c]
        w_t = wt_scr[d, c]
        st_decay = sdec_scr[d, c, edge % SUBLANES:edge % SUBLANES + 1, :]
        xs_c = xs_scr[rows, :]
        xs_b = xs_c.astype(BF16)
        y_parts, st_parts = [], []
        for g in range(N_GROUPS):
            cs = slice(g * GROUP_W, (g + 1) * GROUP_W)
            ns = slice(g * D_STATE, (g + 1) * D_STATE)
            c_g = c_scr[rows, ns]
            bt_g = bt_scr[c, ns, :]
            cb = _dot(c_g.astype(BF16), bt_g.astype(BF16))
            st_g = st_scr[d, :, cs]
            st_b = st_g.astype(BF16)
            x_g = xs_b[:, cs]
            zero = jnp.zeros_like(x_g)
            lhs_y, rhs_y, lhs_s, rhs_s = [], [], [], []
            for hh in range(HEADS_PER_GROUP):
                h = g * HEADS_PER_GROUP + hh
                cum_l = jnp.broadcast_to(cum[:, h:h + 1], (CHUNK, CHUNK))
                seg = jnp.where(causal[d], cum_l - src_t[h:h + 1, :], -jnp.inf)
                x_h = jnp.where(head_lanes[hh], x_g, zero)
                lhs_y.append((cb * jnp.exp2(seg)).astype(BF16))
                rhs_y.append(x_h)
                lhs_y.append((c_g * jnp.exp2(cum_l)).astype(BF16))
                rhs_y.append(jnp.where(head_lanes[hh], st_b, zero))
                lhs_s.append((bt_g * w_t[h:h + 1, :]).astype(BF16))
                rhs_s.append(x_h)
            y_parts.append(_dot(jnp.concatenate(lhs_y, axis=1), jnp.concatenate(rhs_y, axis=0)))
            st_parts.append(st_g * st_decay[:, cs] + _dot(jnp.concatenate(lhs_s, axis=1),
                                                          jnp.concatenate(rhs_s, axis=0)))
        return rows, y_parts, st_parts

    def ssd_step(i, carry):
        results = [ssd_chunk(i, d) for d in range(2)]
        for d, (rows, y_parts, st_parts) in enumerate(results):
            for g in range(N_GROUPS):
                st_scr[d, :, g * GROUP_W:(g + 1) * GROUP_W] = st_parts[g]
                for k in range(GROUP_W // LANES):
                    kt = g * (GROUP_W // LANES) + k
                    y_scr[kt, rows, :] = y_scr[kt, rows, :] + y_parts[g][:, k * LANES:(k + 1) * LANES]
        return carry

    for d in range(2):
        for k in range(D_SSM // D_STATE):
            ks = slice(k * D_STATE, (k + 1) * D_STATE)
            if has_h0:
                st_scr[d, :, ks] = h0_ref[0, 0, d, ks, :].T
            else:
                st_scr[d, :, ks] = jnp.zeros((D_STATE, D_STATE), F32)
    lax.fori_loop(0, nc, ssd_step, 0)
    if want_state:
        for d in range(2):
            for k in range(D_SSM // D_STATE):
                ks = slice(k * D_STATE, (k + 1) * D_STATE)
                st_ref[0, 0, d, ks, :] = st_scr[d, :, ks].T

    n_lane_tiles = D_SSM // LANES
    if col_major:
        rows_per_col = L // GRID_W
        for r in range(rows_per_col):
            for kt in range(n_lane_tiles):
                xs_scr[r * GRID_W:(r + 1) * GRID_W, kt * LANES:(kt + 1) * LANES] = (
                    y_scr[kt, pl.ds(r, GRID_W, stride=rows_per_col), :])

    def tail(t, carry):
        rows = row_tile(t)
        hn = h_scr[rows, :]
        if col_major:
            y_t = xs_scr[rows, :]
        else:
            y_t = jnp.concatenate([y_scr[kt, rows, :] for kt in range(n_lane_tiles)], axis=1)
        yz = y_t * _silu(_dot(hn, wzm_ref[...]))
        ym = (yz * lax.rsqrt(_mean_last(yz * yz) + EPS) * ssmw_ref[...]).astype(BF16)
        merged = (_sigmoid(_dot(hn, wglm_ref[...])) * _dot(ym, woutm_ref[...])
                  + out_ref[0, rows, :])
        o = x_ref[0, rows, :] + gate * _dot(merged.astype(BF16), wout_ref[...])
        if final_norm:
            o = o * lax.rsqrt(_mean_last(o * o) + EPS) * finw_ref[...]
        out_ref[0, rows, :] = o
        return carry

    over_row_tiles(tail)


def _const_spec(shape):
    nd = len(shape)
    return pl.BlockSpec(shape, lambda b, _nd=nd: (0,) * _nd, pipeline_mode=pl.Buffered(1))


def _layer_call(x, mod, h0, weights, *, layer, state_buf, col_major, want_state, final_norm,
                mod_per_seq):
    nb, L, _ = x.shape
    nc = L // CHUNK
    has_h0 = h0 is not None
    state_aliased = want_state and state_buf is not None
    kernel = functools.partial(_layer_kernel, seq_len=L, col_major=col_major, has_h0=has_h0,
                               want_state=want_state, final_norm=final_norm,
                               state_aliased=state_aliased)
    single = pl.Buffered(1) if L * D_MODEL * 4 >= (4 << 20) else None
    state_block = (1, 1, 2, D_SSM, D_STATE)
    state_index = lambda b: (b, layer, 0, 0, 0)
    in_specs = [pl.BlockSpec((1, L, D_MODEL), lambda b: (b, 0, 0), pipeline_mode=single),
                pl.BlockSpec((1, 3, D_MODEL), (lambda b: (b, 0, 0)) if mod_per_seq
                             else (lambda b: (0, 0, 0)))]
    args = [x, mod]
    if has_h0:
        in_specs.append(pl.BlockSpec(state_block, state_index))
        args.append(h0)
    for w in weights:
        in_specs.append(_const_spec(w.shape))
        args.append(w)
    aliases = {}
    if state_aliased:
        aliases[len(args)] = 1
        in_specs.append(pl.BlockSpec(memory_space=pl.ANY))
        args.append(state_buf)
    out_shape = [jax.ShapeDtypeStruct((nb, L, D_MODEL), F32)]
    out_specs = [pl.BlockSpec((1, L, D_MODEL), lambda b: (b, 0, 0), pipeline_mode=single)]
    if want_state:
        out_shape.append(jax.ShapeDtypeStruct((nb, DEPTH, 2, D_SSM, D_STATE), F32))
        out_specs.append(pl.BlockSpec(state_block, state_index))
    scratch = [
        pltpu.VMEM((L, D_MODEL), BF16),
        pltpu.VMEM((L if col_major else SUBLANES * 2, D_MODEL), BF16),
        pltpu.VMEM((2, L + 2 * CONV_PAD, CONV_TILE), F32),
        pltpu.VMEM((L, D_SSM), F32),
        pltpu.VMEM((L, N_GROUPS * D_STATE), F32),
        pltpu.VMEM((nc, N_GROUPS * D_STATE, CHUNK), F32),
        pltpu.VMEM((D_SSM // LANES, L, LANES), F32),
        pltpu.VMEM((2, D_STATE, D_SSM), F32),
        pltpu.VMEM((2, L, LANES), F32),
        pltpu.VMEM((ROW_TILE, D_GMLP), BF16),
        pltpu.VMEM((ROW_TILE, D_GMLP), F32),
        pltpu.VMEM((2, L, LANES), F32),
        pltpu.VMEM((2, nc, N_HEADS, CHUNK), F32),
        pltpu.VMEM((2, nc, N_HEADS, CHUNK), F32),
        pltpu.VMEM((2, nc, SUBLANES, D_SSM), F32),
    ]
    res = pl.pallas_call(
        kernel,
        grid=(nb,),
        in_specs=in_specs,
        out_specs=out_specs,
        out_shape=out_shape,
        scratch_shapes=scratch,
        input_output_aliases=aliases,
        compiler_params=pltpu.CompilerParams(
            dimension_semantics=("arbitrary",), vmem_limit_bytes=VMEM_LIMIT_BYTES),
        name="layer_L%d%s" % (L, "_cm" if col_major else ""),
    )(*args)
    return res if want_state else (res[0], None)


def _prep_layer_weights(i, w_in, norm_w, conv_w, conv_b, dt_bias, a_log, d_skip, ssm_norm_w,
                        w_out_m, sgu_ln_w, sgu_ln_b, w_sp, b_sp, w_out_g, w_out, final_norm_w,
                        expand):
    pts = np.cumsum((0,) + IN_SIZES)
    seg = [w_in[i][:, pts[k]:pts[k + 1]].astype(BF16) for k in range(len(IN_SIZES))]
    w_xbc, w_zm, w_dt, w_u, w_v, w_zg, w_glm, w_glg = seg
    w_dt = w_dt.reshape(D_MODEL, 2, N_HEADS).transpose(1, 0, 2)
    w_dt = jnp.pad(w_dt, ((0, 0), (0, 0), (0, LANES - N_HEADS)))
    pad_h = ((0, 0), (0, LANES - N_HEADS))
    row = lambda v: v.reshape(1, -1)
    return [
        w_xbc, w_zm, w_dt, w_u, w_v, w_zg, w_glm, w_glg,
        w_out_m[i].astype(BF16), w_out_g[i].astype(BF16), w_out[i].astype(BF16),
        w_sp[i].astype(BF16),
        row(norm_w[i]), conv_w[i], row(conv_b[i]),
        jnp.pad(dt_bias[i], pad_h), jnp.pad(a_log[i], pad_h),
        row(jnp.repeat(d_skip[i], HEAD_DIM)), row(ssm_norm_w[i]),
        row(sgu_ln_w[i]), row(sgu_ln_b[i]),
        jnp.repeat(b_sp[i].T, D_GMLP // N_GROUPS_GMLP, axis=1),
        row(final_norm_w), expand,
    ]


def kernel(x_prompt, x_sample, state_ssm, c, c_ctx, w_ada, b_ada, norm_w, w_in, conv_w, conv_b,
           dt_bias, a_log, d_skip, ssm_norm_w, w_out_m, sgu_ln_w, sgu_ln_b, w_sp, b_sp, w_out_g,
           w_out, final_norm_w):
    n_lat = c.shape[0]
    n_ctx = x_prompt.shape[0]
    cc = jnp.concatenate([c_ctx[None, :], c,
                          jnp.zeros((SUBLANES - 1 - n_lat, D_MODEL), F32)], axis=0)
    mod = _modulation(cc, w_ada, b_ada).reshape(DEPTH, SUBLANES, 3, D_MODEL)
    expand = (jnp.arange(LANES)[:, None] == (jnp.arange(D_SSM)[None, :] // HEAD_DIM)).astype(BF16)
    h0_lat = state_ssm.reshape(n_lat, DEPTH, 2, D_SSM, D_STATE)

    h_ctx, h_lat = x_prompt, x_sample
    states = None
    for i in range(DEPTH):
        weights = _prep_layer_weights(i, w_in, norm_w, conv_w, conv_b, dt_bias, a_log, d_skip,
                                      ssm_norm_w, w_out_m, sgu_ln_w, sgu_ln_b, w_sp, b_sp,
                                      w_out_g, w_out, final_norm_w, expand)
        last = i == DEPTH - 1
        h_ctx, states = _layer_call(h_ctx, mod[i, 0:1], None, weights, layer=i, state_buf=states,
                                    col_major=False, want_state=True, final_norm=last,
                                    mod_per_seq=False)
        h_lat, _ = _layer_call(h_lat, mod[i, 1:1 + n_lat], h0_lat, weights, layer=i,
                               state_buf=None, col_major=(i % 2 == 1), want_state=False,
                               final_norm=last, mod_per_seq=True)
    return h_ctx, h_lat, states.reshape(n_ctx, DEPTH, 2, N_HEADS, HEAD_DIM, D_STATE)
```

```python
import functools
import itertools

import jax
import jax.numpy as jnp
import numpy as np
from jax import lax
from jax.experimental import pallas as pl
from jax.experimental.pallas import tpu as pltpu

F32 = jnp.float32
BF16 = jnp.bfloat16

D_MODEL = 1024
DEPTH = 4
GRID_W = 64
D_SSM = 1024
HEAD_DIM = 64
N_HEADS = 16
N_GROUPS = 4
HEADS_PER_GROUP = N_HEADS // N_GROUPS
D_STATE = 128
D_CONV = 5
CHUNK = 128
CONV_DIM = D_SSM + 2 * N_GROUPS * D_STATE
D_GMLP = 1024
N_GROUPS_GMLP = 8
DT_LO = CONV_DIM + D_SSM
DT_HI = DT_LO + 2 * N_HEADS
_SEG_NAMES = ("xbc", "zm", "u", "v", "zg", "glm", "glg")
_SEG_ENDS = np.cumsum((CONV_DIM, D_SSM, D_GMLP, D_GMLP, D_GMLP, D_MODEL, D_MODEL))
W_IN_SEGMENTS = {n: (int(hi - sz), int(hi)) for n, hi, sz in zip(
    _SEG_NAMES, _SEG_ENDS, (CONV_DIM, D_SSM, D_GMLP, D_GMLP, D_GMLP, D_MODEL, D_MODEL))}
EPS = 1e-6

LANES = 128
SUBLANES = 8
CONV_PAD = SUBLANES
CONV_TILE = 256
GROUP_W = HEADS_PER_GROUP * HEAD_DIM
ROW_TILE = 256
VMEM_LIMIT_BYTES = 60 * 1024 * 1024
LOG2_E = np.float32(1.4426950408889634)


def _dot(a, b):
    return jnp.dot(a, b, preferred_element_type=F32)


def _split_bf16(a, parts):
    out = []
    rem = a
    for i in range(parts):
        p = rem.astype(BF16)
        out.append(p)
        if i + 1 < parts:
            rem = rem - p.astype(F32)
    return out


def _dot_exact_rhs(a, m_bf16, parts):
    acc = None
    for p in _split_bf16(a, parts):
        t = _dot(p, m_bf16)
        acc = t if acc is None else acc + t
    return acc


def _dot_exact_lhs(m_bf16, a, parts):
    acc = None
    for p in _split_bf16(a, parts):
        t = _dot(m_bf16, p)
        acc = t if acc is None else acc + t
    return acc


def _sigmoid(x):
    return 1.0 / (1.0 + jnp.exp(-x))


def _silu(x):
    return x * _sigmoid(x)


def _gelu_tanh(x):
    c = np.float32(np.sqrt(2.0 / np.pi))
    return x * (0.5 * (1.0 + jnp.tanh(c * (x + 0.044715 * (x * x * x)))))


def _softplus(x):
    return jnp.maximum(x, 0.0) + jnp.log1p(jnp.exp(-jnp.abs(x)))


def _mean_last(x):
    return jnp.mean(x, axis=-1, keepdims=True)


def _mod_kernel(c_ref, w_ref, b_ref, o_ref):
    sc = _silu(c_ref[...]).astype(BF16)
    o_ref[0] = _dot(sc, w_ref[0].astype(BF16)) + b_ref[0]


def _modulation(cc, w_ada, b_ada):
    rows = cc.shape[0]
    ncol = 3 * D_MODEL // D_MODEL
    return pl.pallas_call(
        _mod_kernel,
        grid=(DEPTH, ncol),
        in_specs=[
            pl.BlockSpec((rows, D_MODEL), lambda i, j: (0, 0)),
            pl.BlockSpec((1, D_MODEL, D_MODEL), lambda i, j: (i, 0, j)),
            pl.BlockSpec((1, 1, D_MODEL), lambda i, j: (i, 0, j)),
        ],
        out_specs=pl.BlockSpec((1, rows, D_MODEL), lambda i, j: (i, 0, j)),
        out_shape=jax.ShapeDtypeStruct((DEPTH, rows, 3 * D_MODEL), F32),
        name="adaln_mod",
    )(cc, w_ada, b_ada.reshape(DEPTH, 1, 3 * D_MODEL))


def _layer_kernel(*refs, seq_len, col_major, has_h0, want_state, final_norm, state_aliased):
    L = seq_len
    nc = L // CHUNK
    it = iter(refs)
    x_ref = next(it)
    mod_ref = next(it)
    h0_ref = next(it) if has_h0 else None
    (win_ref, wdt_ref, woutm_ref, woutg_ref, wout_ref, wsp_ref,
     normw_ref, convw_ref, convb_ref, dtb_ref, alog_ref, dskip_ref, ssmw_ref,
     lnw_ref, lnb_ref, bsp_ref, finw_ref, expand_ref) = (next(it) for _ in range(18))
    w_seg = {name: win_ref.at[0, :, lo:hi] for name, (lo, hi) in W_IN_SEGMENTS.items()}
    wxbc_ref, wzm_ref, wu_ref, wv_ref = w_seg["xbc"], w_seg["zm"], w_seg["u"], w_seg["v"]
    wzg_ref, wglm_ref, wglg_ref = w_seg["zg"], w_seg["glm"], w_seg["glg"]
    wdt_ref, woutm_ref, woutg_ref, wout_ref, wsp_ref = (
        r.at[0] for r in (wdt_ref, woutm_ref, woutg_ref, wout_ref, wsp_ref))
    (normw_ref, convw_ref, convb_ref, dtb_ref, alog_ref, dskip_ref, ssmw_ref, lnw_ref, lnb_ref,
     bsp_ref) = (r.at[0] for r in (normw_ref, convw_ref, convb_ref, dtb_ref, alog_ref, dskip_ref,
                                   ssmw_ref, lnw_ref, lnb_ref, bsp_ref))
    if state_aliased:
        next(it)
    out_ref = next(it)
    st_ref = next(it) if want_state else None
    (h_scr, hcm_scr, xraw_scr, xs_scr, c_scr, bt_scr, y_scr, st_scr,
     dt_scr, v_scr, s_scr, cum_scr, srct_scr, wt_scr, sdec_scr) = (next(it) for _ in range(15))

    shift = mod_ref[0, 0, 0:1, :]
    scale = mod_ref[0, 0, 1:2, :]
    gate = mod_ref[0, 0, 2:3, :]
    normw = normw_ref[...]
    n_row_tiles = L // ROW_TILE

    def norm_mod(xt):
        y = xt * lax.rsqrt(_mean_last(xt * xt) + EPS) * normw
        return (y * (1.0 + scale) + shift).astype(BF16)

    def row_tile(t):
        if isinstance(t, int):
            return slice(t * ROW_TILE, (t + 1) * ROW_TILE)
        return pl.ds(pl.multiple_of(t * ROW_TILE, ROW_TILE), ROW_TILE)

    def over_row_tiles(body):
        if n_row_tiles == 1:
            body(0, 0)
        else:
            lax.fori_loop(0, n_row_tiles, body, 0)

    for r0 in range(0, L, ROW_TILE):
        h_scr[r0:r0 + ROW_TILE, :] = norm_mod(x_ref[0, r0:r0 + ROW_TILE, :])
    if col_major:
        rows_per_col = L // GRID_W
        for r0 in range(0, L, ROW_TILE):
            dst = r0 + lax.broadcasted_iota(jnp.int32, (ROW_TILE, L), 0)
            src = lax.broadcasted_iota(jnp.int32, (ROW_TILE, L), 1)
            want = (dst % rows_per_col) * GRID_W + dst // rows_per_col
            perm = jnp.where(src == want, 1.0, 0.0).astype(BF16)
            hcm_scr[r0:r0 + ROW_TILE, :] = _dot(perm, h_scr[...]).astype(BF16)
        hsrc = hcm_scr
    else:
        hsrc = h_scr

    def gmlp_steps(t):
        rows = row_tile(t)
        hn = h_scr[rows, :]
        v = _gelu_tanh(_dot(hn, wv_ref[...]))
        vc = v - _mean_last(v)
        v_scr[...] = (vc * lax.rsqrt(_mean_last(vc * vc) + EPS) * lnw_ref[...]
                      + lnb_ref[...]).astype(BF16)
        yield
        for cc in range(ROW_TILE // CHUNK):
            for g in range(N_GROUPS_GMLP):
                blk = (slice(cc * CHUNK, (cc + 1) * CHUNK), slice(g * CHUNK, (g + 1) * CHUNK))
                s_scr[blk] = _dot(wsp_ref[g], v_scr[blk]) + bsp_ref[:, blk[1]]
            yield
        u = _gelu_tanh(_dot(hn, wu_ref[...]))
        yield
        yg = (u * s_scr[...] * _silu(_dot(hn, wzg_ref[...]))).astype(BF16)
        yield
        og = _dot(yg, woutg_ref[...])
        yield
        out_ref[0, rows, :] = _sigmoid(_dot(hn, wglg_ref[...])) * og
        yield

    def gmlp_tile(t, carry):
        for _ in gmlp_steps(t):
            pass
        return carry

    zero_pad = jnp.zeros((CONV_PAD, CONV_TILE), F32)
    for buf in range(2):
        xraw_scr[buf, 0:CONV_PAD, :] = zero_pad
        xraw_scr[buf, CONV_PAD + L:CONV_PAD + L + CONV_PAD, :] = zero_pad
    half = D_CONV // 2
    n_conv_tiles = CONV_DIM // CONV_TILE
    b_tile0 = D_SSM // CONV_TILE
    c_tile0 = (D_SSM + N_GROUPS * D_STATE) // CONV_TILE

    def project_conv_tile(j):
        cols = slice(j * CONV_TILE, (j + 1) * CONV_TILE)
        xraw_scr[j % 2, CONV_PAD:CONV_PAD + L, :] = _dot(hsrc[...], wxbc_ref[:, cols])

    def conv_tile(j):
        cols = slice(j * CONV_TILE, (j + 1) * CONV_TILE)
        for r0 in range(0, L, ROW_TILE):
            acc = convb_ref[:, cols]
            for k in range(D_CONV):
                start = CONV_PAD + r0 + k - half
                acc = acc + convw_ref[k:k + 1, cols] * xraw_scr[j % 2, start:start + ROW_TILE, :]
            act = _silu(acc)
            if j < b_tile0:
                xs_scr[r0:r0 + ROW_TILE, cols] = act
                skip = dskip_ref[:, cols] * act
                for k in range(CONV_TILE // LANES):
                    y_scr[j * (CONV_TILE // LANES) + k, r0:r0 + ROW_TILE, :] = (
                        skip[:, k * LANES:(k + 1) * LANES])
            elif j < c_tile0:
                bcols = (j - b_tile0) * CONV_TILE
                for cc in range(ROW_TILE // CHUNK):
                    for gg in range(CONV_TILE // D_STATE):
                        blk = act[cc * CHUNK:(cc + 1) * CHUNK, gg * D_STATE:(gg + 1) * D_STATE]
                        bt_scr[r0 // CHUNK + cc,
                               bcols + gg * D_STATE:bcols + (gg + 1) * D_STATE, :] = blk.T
            else:
                ccols = (j - c_tile0) * CONV_TILE
                c_scr[r0:r0 + ROW_TILE, ccols:ccols + CONV_TILE] = act

    def conv_steps():
        project_conv_tile(0)
        for j in range(n_conv_tiles):
            if j + 1 < n_conv_tiles:
                project_conv_tile(j + 1)
            conv_tile(j)
            yield

    def dt_steps():
        for d in range(2):
            dt_scr[d] = _softplus(_dot(hsrc[...], wdt_ref[d]) + dtb_ref[d:d + 1, :])
        yield

    if n_row_tiles == 1:
        for _ in itertools.zip_longest(itertools.chain(gmlp_steps(0), dt_steps()), conv_steps()):
            pass
    else:
        lax.fori_loop(0, n_row_tiles, gmlp_tile, 0)
        for _ in itertools.chain(conv_steps(), dt_steps()):
            pass

    lane = lax.broadcasted_iota(jnp.int32, (1, LANES), 1)
    a_rows = [jnp.where(lane < N_HEADS, -jnp.exp(alog_ref[d:d + 1, :]) * LOG2_E, 0.0)
              for d in range(2)]

    row_i = lax.broadcasted_iota(jnp.int32, (CHUNK, CHUNK), 0)
    col_i = lax.broadcasted_iota(jnp.int32, (CHUNK, CHUNK), 1)
    causal = [row_i >= col_i, row_i <= col_i]
    tri = [jnp.where(m, 1.0, 0.0).astype(BF16) for m in causal]
    glane = lax.broadcasted_iota(jnp.int32, (CHUNK, GROUP_W), 1)
    head_lanes = [(glane >= hh * HEAD_DIM) & (glane < (hh + 1) * HEAD_DIM)
                  for hh in range(HEADS_PER_GROUP)]
    expand = expand_ref[...]

    tables = []
    for d in range(2):
        for c in range(nc):
            rows = slice(c * CHUNK, (c + 1) * CHUNK)
            dt_c = dt_scr[d, rows, :]
            tables.append((d, c, rows, dt_c, _dot_exact_lhs(tri[d], dt_c * a_rows[d], 3)))
    for d, c, rows, dt_c, cum in tables:
        edge = CHUNK - 1 if d == 0 else 0
        edge_slab = slice(CHUNK - SUBLANES, CHUNK) if d == 0 else slice(0, SUBLANES)
        cum_scr[d, rows, :] = cum
        srct_scr[d, c] = (cum.T - jnp.log2(dt_c.T))[0:N_HEADS, :]
        wt_scr[d, c] = (dt_c * jnp.exp2(cum[edge:edge + 1, :] - cum)).T[0:N_HEADS, :]
        sdec_scr[d, c] = _dot_exact_rhs(jnp.exp2(cum[edge_slab, :]), expand, 2)

    def ssd_chunk(i, d):
        c = i if d == 0 else nc - 1 - i
        rows = pl.ds(pl.multiple_of(c * CHUNK, CHUNK), CHUNK)
        edge = CHUNK - 1 if d == 0 else 0
        cum = cum_scr[d, rows, :]
        src_t = srct_scr[d, c]
        w_t = wt_scr[d, c]
        st_decay = sdec_scr[d, c, edge % SUBLANES:edge % SUBLANES + 1, :]
        xs_c = xs_scr[rows, :]
        xs_b = xs_c.astype(BF16)
        y_parts, st_parts = [], []
        for g in range(N_GROUPS):
            cs = slice(g * GROUP_W, (g + 1) * GROUP_W)
            ns = slice(g * D_STATE, (g + 1) * D_STATE)
            c_g = c_scr[rows, ns]
            bt_g = bt_scr[c, ns, :]
            cb = _dot(c_g.astype(BF16), bt_g.astype(BF16))
            st_g = st_scr[d, :, cs]
            st_b = st_g.astype(BF16)
            x_g = xs_b[:, cs]
            zero = jnp.zeros_like(x_g)
            lhs_y, rhs_y, lhs_s, rhs_s = [], [], [], []
            for hh in range(HEADS_PER_GROUP):
                h = g * HEADS_PER_GROUP + hh
                cum_l = jnp.broadcast_to(cum[:, h:h + 1], (CHUNK, CHUNK))
                seg = jnp.where(causal[d], cum_l - src_t[h:h + 1, :], -jnp.inf)
                x_h = jnp.where(head_lanes[hh], x_g, zero)
                lhs_y.append((cb * jnp.exp2(seg)).astype(BF16))
                rhs_y.append(x_h)
                lhs_y.append((c_g * jnp.exp2(cum_l)).astype(BF16))
                rhs_y.append(jnp.where(head_lanes[hh], st_b, zero))
                lhs_s.append((bt_g * w_t[h:h + 1, :]).astype(BF16))
                rhs_s.append(x_h)
            y_parts.append(_dot(jnp.concatenate(lhs_y, axis=1), jnp.concatenate(rhs_y, axis=0)))
            st_parts.append(st_g * st_decay[:, cs] + _dot(jnp.concatenate(lhs_s, axis=1),
                                                          jnp.concatenate(rhs_s, axis=0)))
        return rows, y_parts, st_parts

    def ssd_step(i, carry):
        results = [ssd_chunk(i, d) for d in range(2)]
        for d, (rows, y_parts, st_parts) in enumerate(results):
            for g in range(N_GROUPS):
                st_scr[d, :, g * GROUP_W:(g + 1) * GROUP_W] = st_parts[g]
                for k in range(GROUP_W // LANES):
                    kt = g * (GROUP_W // LANES) + k
                    y_scr[kt, rows, :] = y_scr[kt, rows, :] + y_parts[g][:, k * LANES:(k + 1) * LANES]
        return carry

    for d in range(2):
        for k in range(D_SSM // D_STATE):
            ks = slice(k * D_STATE, (k + 1) * D_STATE)
            if has_h0:
                st_scr[d, :, ks] = h0_ref[0, 0, d, ks, :].T
            else:
                st_scr[d, :, ks] = jnp.zeros((D_STATE, D_STATE), F32)
    lax.fori_loop(0, nc, ssd_step, 0)
    if want_state:
        for d in range(2):
            for k in range(D_SSM // D_STATE):
                ks = slice(k * D_STATE, (k + 1) * D_STATE)
                st_ref[0, 0, d, ks, :] = st_scr[d, :, ks].T

    n_lane_tiles = D_SSM // LANES
    if col_major:
        rows_per_col = L // GRID_W
        for r in range(rows_per_col):
            for kt in range(n_lane_tiles):
                xs_scr[r * GRID_W:(r + 1) * GRID_W, kt * LANES:(kt + 1) * LANES] = (
                    y_scr[kt, pl.ds(r, GRID_W, stride=rows_per_col), :])

    def tail(t, carry):
        rows = row_tile(t)
        hn = h_scr[rows, :]
        if col_major:
            y_t = xs_scr[rows, :]
        else:
            y_t = jnp.concatenate([y_scr[kt, rows, :] for kt in range(n_lane_tiles)], axis=1)
        yz = y_t * _silu(_dot(hn, wzm_ref[...]))
        ym = (yz * lax.rsqrt(_mean_last(yz * yz) + EPS) * ssmw_ref[...]).astype(BF16)
        merged = (_sigmoid(_dot(hn, wglm_ref[...])) * _dot(ym, woutm_ref[...])
                  + out_ref[0, rows, :])
        o = x_ref[0, rows, :] + gate * _dot(merged.astype(BF16), wout_ref[...])
        if final_norm:
            o = o * lax.rsqrt(_mean_last(o * o) + EPS) * finw_ref[...]
        out_ref[0, rows, :] = o
        return carry

    over_row_tiles(tail)


def _param_spec(arr, layer):
    nd = arr.ndim
    if layer is None:
        return pl.BlockSpec(arr.shape, lambda b: (0,) * nd, pipeline_mode=pl.Buffered(1))
    return pl.BlockSpec((1,) + arr.shape[1:], lambda b: (layer,) + (0,) * (nd - 1),
                        pipeline_mode=pl.Buffered(1))


def _layer_call(x, mod, h0, weights, *, layer, state_buf, col_major, want_state, final_norm,
                mod_per_seq):
    nb, L, _ = x.shape
    nc = L // CHUNK
    has_h0 = h0 is not None
    state_aliased = want_state and state_buf is not None
    kernel = functools.partial(_layer_kernel, seq_len=L, col_major=col_major, has_h0=has_h0,
                               want_state=want_state, final_norm=final_norm,
                               state_aliased=state_aliased)
    single = pl.Buffered(1) if L * D_MODEL * 4 >= (4 << 20) else None
    state_block = (1, 1, 2, D_SSM, D_STATE)
    state_index = lambda b: (b, layer, 0, 0, 0)
    in_specs = [pl.BlockSpec((1, L, D_MODEL), lambda b: (b, 0, 0), pipeline_mode=single),
                pl.BlockSpec((1, 1, 3, D_MODEL), (lambda b: (layer, 1 + b, 0, 0)) if mod_per_seq
                             else (lambda b: (layer, 0, 0, 0)))]
    args = [x, mod]
    if has_h0:
        in_specs.append(pl.BlockSpec(state_block, state_index))
        args.append(h0)
    stacked, shared = weights
    for w in stacked:
        in_specs.append(_param_spec(w, layer))
        args.append(w)
    for w in shared:
        in_specs.append(_param_spec(w, None))
        args.append(w)
    aliases = {}
    if state_aliased:
        aliases[len(args)] = 1
        in_specs.append(pl.BlockSpec(memory_space=pl.ANY))
        args.append(state_buf)
    out_shape = [jax.ShapeDtypeStruct((nb, L, D_MODEL), F32)]
    out_specs = [pl.BlockSpec((1, L, D_MODEL), lambda b: (b, 0, 0), pipeline_mode=single)]
    if want_state:
        out_shape.append(jax.ShapeDtypeStruct((nb, DEPTH, 2, D_SSM, D_STATE), F32))
        out_specs.append(pl.BlockSpec(state_block, state_index))
    scratch = [
        pltpu.VMEM((L, D_MODEL), BF16),
        pltpu.VMEM((L if col_major else SUBLANES * 2, D_MODEL), BF16),
        pltpu.VMEM((2, L + 2 * CONV_PAD, CONV_TILE), F32),
        pltpu.VMEM((L, D_SSM), F32),
        pltpu.VMEM((L, N_GROUPS * D_STATE), F32),
        pltpu.VMEM((nc, N_GROUPS * D_STATE, CHUNK), F32),
        pltpu.VMEM((D_SSM // LANES, L, LANES), F32),
        pltpu.VMEM((2, D_STATE, D_SSM), F32),
        pltpu.VMEM((2, L, LANES), F32),
        pltpu.VMEM((ROW_TILE, D_GMLP), BF16),
        pltpu.VMEM((ROW_TILE, D_GMLP), F32),
        pltpu.VMEM((2, L, LANES), F32),
        pltpu.VMEM((2, nc, N_HEADS, CHUNK), F32),
        pltpu.VMEM((2, nc, N_HEADS, CHUNK), F32),
        pltpu.VMEM((2, nc, SUBLANES, D_SSM), F32),
    ]
    res = pl.pallas_call(
        kernel,
        grid=(nb,),
        in_specs=in_specs,
        out_specs=out_specs,
        out_shape=out_shape,
        scratch_shapes=scratch,
        input_output_aliases=aliases,
        compiler_params=pltpu.CompilerParams(
            dimension_semantics=("arbitrary",), vmem_limit_bytes=VMEM_LIMIT_BYTES),
        name="layer_L%d%s" % (L, "_cm" if col_major else ""),
    )(*args)
    return res if want_state else (res[0], None)


def _prep_weights(w_in, norm_w, conv_w, conv_b, dt_bias, a_log, d_skip, ssm_norm_w, w_out_m,
                  sgu_ln_w, sgu_ln_b, w_sp, b_sp, w_out_g, w_out, final_norm_w):
    w_main = jnp.concatenate([w_in[:, :, :DT_LO], w_in[:, :, DT_HI:]], axis=-1).astype(BF16)
    w_dt = w_in[:, :, DT_LO:DT_HI].astype(BF16).reshape(DEPTH, D_MODEL, 2, N_HEADS)
    w_dt = jnp.pad(w_dt.transpose(0, 2, 1, 3), ((0, 0), (0, 0), (0, 0), (0, LANES - N_HEADS)))
    pad_h = ((0, 0), (0, 0), (0, LANES - N_HEADS))
    rows = lambda v: v.reshape(DEPTH, 1, -1)
    expand = (jnp.arange(LANES)[:, None] == (jnp.arange(D_SSM)[None, :] // HEAD_DIM)).astype(BF16)
    stacked = [
        w_main, w_dt, w_out_m.astype(BF16), w_out_g.astype(BF16), w_out.astype(BF16),
        w_sp.astype(BF16),
        rows(norm_w), conv_w, rows(conv_b),
        jnp.pad(dt_bias, pad_h), jnp.pad(a_log, pad_h),
        rows(jnp.repeat(d_skip, HEAD_DIM, axis=-1)), rows(ssm_norm_w),
        rows(sgu_ln_w), rows(sgu_ln_b),
        jnp.repeat(b_sp.transpose(0, 2, 1), D_GMLP // N_GROUPS_GMLP, axis=-1),
    ]
    return stacked, [final_norm_w.reshape(1, -1), expand]


def kernel(x_prompt, x_sample, state_ssm, c, c_ctx, w_ada, b_ada, norm_w, w_in, conv_w, conv_b,
           dt_bias, a_log, d_skip, ssm_norm_w, w_out_m, sgu_ln_w, sgu_ln_b, w_sp, b_sp, w_out_g,
           w_out, final_norm_w):
    n_lat = c.shape[0]
    n_ctx = x_prompt.shape[0]
    cc = jnp.concatenate([c_ctx[None, :], c,
                          jnp.zeros((SUBLANES - 1 - n_lat, D_MODEL), F32)], axis=0)
    mod = _modulation(cc, w_ada, b_ada).reshape(DEPTH, SUBLANES, 3, D_MODEL)
    h0_lat = state_ssm.reshape(n_lat, DEPTH, 2, D_SSM, D_STATE)
    weights = _prep_weights(w_in, norm_w, conv_w, conv_b, dt_bias, a_log, d_skip, ssm_norm_w,
                            w_out_m, sgu_ln_w, sgu_ln_b, w_sp, b_sp, w_out_g, w_out, final_norm_w)

    h_ctx, h_lat = x_prompt, x_sample
    states = None
    for i in range(DEPTH):
        last = i == DEPTH - 1
        h_ctx, states = _layer_call(h_ctx, mod, None, weights, layer=i, state_buf=states,
                                    col_major=False, want_state=True, final_norm=last,
                                    mod_per_seq=False)
        h_lat, _ = _layer_call(h_lat, mod, h0_lat, weights, layer=i,
                               state_buf=None, col_major=(i % 2 == 1), want_state=False,
                               final_norm=last, mod_per_seq=True)
    return h_ctx, h_lat, states.reshape(n_ctx, DEPTH, 2, N_HEADS, HEAD_DIM, D_STATE)
```

```python
import functools
import itertools

import jax
import jax.numpy as jnp
import numpy as np
from jax import lax
from jax.experimental import pallas as pl
from jax.experimental.pallas import tpu as pltpu

F32 = jnp.float32
BF16 = jnp.bfloat16

D_MODEL = 1024
DEPTH = 4
GRID_W = 64
D_SSM = 1024
HEAD_DIM = 64
N_HEADS = 16
N_GROUPS = 4
HEADS_PER_GROUP = N_HEADS // N_GROUPS
D_STATE = 128
D_CONV = 5
CHUNK = 128
CONV_DIM = D_SSM + 2 * N_GROUPS * D_STATE
D_GMLP = 1024
N_GROUPS_GMLP = 8
DT_LO = CONV_DIM + D_SSM
DT_HI = DT_LO + 2 * N_HEADS
_A_NAMES, _A_SIZES = ("xbc", "zm"), (CONV_DIM, D_SSM)
_B_NAMES, _B_SIZES = ("u", "v", "zg", "glm", "glg"), (D_GMLP, D_GMLP, D_GMLP, D_MODEL, D_MODEL)
_spans = lambda names, sizes: {n: (int(hi - sz), int(hi))
                               for n, hi, sz in zip(names, np.cumsum(sizes), sizes)}
W_A_SEGMENTS = _spans(_A_NAMES, _A_SIZES)
W_B_SEGMENTS = _spans(_B_NAMES, _B_SIZES)
EPS = 1e-6

LANES = 128
SUBLANES = 8
CONV_PAD = SUBLANES
CONV_TILE = 256
GROUP_W = HEADS_PER_GROUP * HEAD_DIM
ROW_TILE = 256
CAST_ROWS = 256
VMEM_LIMIT_BYTES = 60 * 1024 * 1024
LOG2_E = np.float32(1.4426950408889634)


def _dot(a, b):
    return jnp.dot(a, b, preferred_element_type=F32)


def _split_bf16(a, parts):
    out = []
    rem = a
    for i in range(parts):
        p = rem.astype(BF16)
        out.append(p)
        if i + 1 < parts:
            rem = rem - p.astype(F32)
    return out


def _dot_exact_rhs(a, m_bf16, parts):
    acc = None
    for p in _split_bf16(a, parts):
        t = _dot(p, m_bf16)
        acc = t if acc is None else acc + t
    return acc


def _dot_exact_lhs(m_bf16, a, parts):
    acc = None
    for p in _split_bf16(a, parts):
        t = _dot(m_bf16, p)
        acc = t if acc is None else acc + t
    return acc


def _sigmoid(x):
    return 1.0 / (1.0 + jnp.exp(-x))


def _silu(x):
    return x * _sigmoid(x)


def _gelu_tanh(x):
    c = np.float32(np.sqrt(2.0 / np.pi))
    return x * (0.5 * (1.0 + jnp.tanh(c * (x + 0.044715 * (x * x * x)))))


def _softplus(x):
    return jnp.maximum(x, 0.0) + jnp.log1p(jnp.exp(-jnp.abs(x)))


def _mean_last(x):
    return jnp.mean(x, axis=-1, keepdims=True)


def _mod_kernel(c_ref, w_ref, b_ref, o_ref):
    sc = _silu(c_ref[...]).astype(BF16)
    o_ref[0] = _dot(sc, w_ref[0].astype(BF16)) + b_ref[0]


def _modulation(cc, w_ada, b_ada):
    rows = cc.shape[0]
    ncol = 3 * D_MODEL // D_MODEL
    return pl.pallas_call(
        _mod_kernel,
        grid=(DEPTH, ncol),
        in_specs=[
            pl.BlockSpec((rows, D_MODEL), lambda i, j: (0, 0)),
            pl.BlockSpec((1, D_MODEL, D_MODEL), lambda i, j: (i, 0, j)),
            pl.BlockSpec((1, 1, D_MODEL), lambda i, j: (i, 0, j)),
        ],
        out_specs=pl.BlockSpec((1, rows, D_MODEL), lambda i, j: (i, 0, j)),
        out_shape=jax.ShapeDtypeStruct((DEPTH, rows, 3 * D_MODEL), F32),
        name="adaln_mod",
    )(cc, w_ada, b_ada.reshape(DEPTH, 1, 3 * D_MODEL))


def _cast_kernel(win_ref, wm_ref, wg_ref, wo_ref, a_ref, b_ref, om_ref, og_ref, oo_ref):
    a_ref[0] = win_ref[0, :, 0:DT_LO].astype(BF16)
    b_ref[0] = win_ref[0, :, DT_HI:].astype(BF16)
    om_ref[0] = wm_ref[0].astype(BF16)
    og_ref[0] = wg_ref[0].astype(BF16)
    oo_ref[0] = wo_ref[0].astype(BF16)


def _cast_weights(w_in, w_out_m, w_out_g, w_out):
    d_in = w_in.shape[-1]
    blk = lambda width: pl.BlockSpec((1, CAST_ROWS, width), lambda i, r: (i, r, 0))
    shape = lambda width: jax.ShapeDtypeStruct((DEPTH, D_MODEL, width), BF16)
    return pl.pallas_call(
        _cast_kernel,
        grid=(DEPTH, D_MODEL // CAST_ROWS),
        in_specs=[blk(d_in), blk(D_MODEL), blk(D_MODEL), blk(D_MODEL)],
        out_specs=[blk(DT_LO), blk(d_in - DT_HI), blk(D_MODEL), blk(D_MODEL), blk(D_MODEL)],
        out_shape=[shape(DT_LO), shape(d_in - DT_HI), shape(D_MODEL), shape(D_MODEL),
                   shape(D_MODEL)],
        compiler_params=pltpu.CompilerParams(vmem_limit_bytes=VMEM_LIMIT_BYTES),
        name="cast_weights",
    )(w_in, w_out_m, w_out_g, w_out)


def _layer_kernel(*refs, seq_len, col_major, has_h0, want_state, final_norm, state_aliased):
    L = seq_len
    nc = L // CHUNK
    it = iter(refs)
    x_ref = next(it)
    mod_ref = next(it)
    h0_ref = next(it) if has_h0 else None
    (wa_ref, wb_ref, wdt_ref, woutm_ref, woutg_ref, wout_ref, wsp_ref,
     normw_ref, convw_ref, convb_ref, dtb_ref, alog_ref, dskip_ref, ssmw_ref,
     lnw_ref, lnb_ref, bsp_ref, finw_ref, expand_ref) = (next(it) for _ in range(19))
    w_seg = {name: wa_ref.at[0, :, lo:hi] for name, (lo, hi) in W_A_SEGMENTS.items()}
    w_seg.update({name: wb_ref.at[0, :, lo:hi] for name, (lo, hi) in W_B_SEGMENTS.items()})
    wxbc_ref, wzm_ref, wu_ref, wv_ref = w_seg["xbc"], w_seg["zm"], w_seg["u"], w_seg["v"]
    wzg_ref, wglm_ref, wglg_ref = w_seg["zg"], w_seg["glm"], w_seg["glg"]
    wdt_ref, woutm_ref, woutg_ref, wout_ref, wsp_ref = (
        r.at[0] for r in (wdt_ref, woutm_ref, woutg_ref, wout_ref, wsp_ref))
    (normw_ref, convw_ref, convb_ref, dtb_ref, alog_ref, dskip_ref, ssmw_ref, lnw_ref, lnb_ref,
     bsp_ref) = (r.at[0] for r in (normw_ref, convw_ref, convb_ref, dtb_ref, alog_ref, dskip_ref,
                                   ssmw_ref, lnw_ref, lnb_ref, bsp_ref))
    if state_aliased:
        next(it)
    out_ref = next(it)
    st_ref = next(it) if want_state else None
    (h_scr, hcm_scr, xraw_scr, xs_scr, c_scr, bt_scr, y_scr, st_scr,
     dt_scr, v_scr, s_scr, cum_scr, srct_scr, wt_scr, sdec_scr) = (next(it) for _ in range(15))

    shift = mod_ref[0, 0, 0:1, :]
    scale = mod_ref[0, 0, 1:2, :]
    gate = mod_ref[0, 0, 2:3, :]
    normw = normw_ref[...]
    n_row_tiles = L // ROW_TILE

    def norm_mod(xt):
        y = xt * lax.rsqrt(_mean_last(xt * xt) + EPS) * normw
        return (y * (1.0 + scale) + shift).astype(BF16)

    def row_tile(t):
        if isinstance(t, int):
            return slice(t * ROW_TILE, (t + 1) * ROW_TILE)
        return pl.ds(pl.multiple_of(t * ROW_TILE, ROW_TILE), ROW_TILE)

    def over_row_tiles(body):
        if n_row_tiles == 1:
            body(0, 0)
        else:
            lax.fori_loop(0, n_row_tiles, body, 0)

    for r0 in range(0, L, ROW_TILE):
        h_scr[r0:r0 + ROW_TILE, :] = norm_mod(x_ref[0, r0:r0 + ROW_TILE, :])
    if col_major:
        rows_per_col = L // GRID_W
        for r0 in range(0, L, ROW_TILE):
            dst = r0 + lax.broadcasted_iota(jnp.int32, (ROW_TILE, L), 0)
            src = lax.broadcasted_iota(jnp.int32, (ROW_TILE, L), 1)
            want = (dst % rows_per_col) * GRID_W + dst // rows_per_col
            perm = jnp.where(src == want, 1.0, 0.0).astype(BF16)
            hcm_scr[r0:r0 + ROW_TILE, :] = _dot(perm, h_scr[...]).astype(BF16)
        hsrc = hcm_scr
    else:
        hsrc = h_scr

    def gmlp_steps(t):
        rows = row_tile(t)
        hn = h_scr[rows, :]
        v = _gelu_tanh(_dot(hn, wv_ref[...]))
        vc = v - _mean_last(v)
        v_scr[...] = (vc * lax.rsqrt(_mean_last(vc * vc) + EPS) * lnw_ref[...]
                      + lnb_ref[...]).astype(BF16)
        yield
        for cc in range(ROW_TILE // CHUNK):
            for g in range(N_GROUPS_GMLP):
                blk = (slice(cc * CHUNK, (cc + 1) * CHUNK), slice(g * CHUNK, (g + 1) * CHUNK))
                s_scr[blk] = _dot(wsp_ref[g], v_scr[blk]) + bsp_ref[:, blk[1]]
            yield
        u = _gelu_tanh(_dot(hn, wu_ref[...]))
        yield
        yg = (u * s_scr[...] * _silu(_dot(hn, wzg_ref[...]))).astype(BF16)
        yield
        og = _dot(yg, woutg_ref[...])
        yield
        out_ref[0, rows, :] = _sigmoid(_dot(hn, wglg_ref[...])) * og
        yield

    def gmlp_tile(t, carry):
        for _ in gmlp_steps(t):
            pass
        return carry

    zero_pad = jnp.zeros((CONV_PAD, CONV_TILE), F32)
    for buf in range(2):
        xraw_scr[buf, 0:CONV_PAD, :] = zero_pad
        xraw_scr[buf, CONV_PAD + L:CONV_PAD + L + CONV_PAD, :] = zero_pad
    half = D_CONV // 2
    n_conv_tiles = CONV_DIM // CONV_TILE
    b_tile0 = D_SSM // CONV_TILE
    c_tile0 = (D_SSM + N_GROUPS * D_STATE) // CONV_TILE

    def project_conv_tile(j):
        cols = slice(j * CONV_TILE, (j + 1) * CONV_TILE)
        xraw_scr[j % 2, CONV_PAD:CONV_PAD + L, :] = _dot(hsrc[...], wxbc_ref[:, cols])

    def conv_tile(j):
        cols = slice(j * CONV_TILE, (j + 1) * CONV_TILE)
        for r0 in range(0, L, ROW_TILE):
            acc = convb_ref[:, cols]
            for k in range(D_CONV):
                start = CONV_PAD + r0 + k - half
                acc = acc + convw_ref[k:k + 1, cols] * xraw_scr[j % 2, start:start + ROW_TILE, :]
            act = _silu(acc)
            if j < b_tile0:
                xs_scr[r0:r0 + ROW_TILE, cols] = act
                skip = dskip_ref[:, cols] * act
                for k in range(CONV_TILE // LANES):
                    y_scr[j * (CONV_TILE // LANES) + k, r0:r0 + ROW_TILE, :] = (
                        skip[:, k * LANES:(k + 1) * LANES])
            elif j < c_tile0:
                bcols = (j - b_tile0) * CONV_TILE
                for cc in range(ROW_TILE // CHUNK):
                    for gg in range(CONV_TILE // D_STATE):
                        blk = act[cc * CHUNK:(cc + 1) * CHUNK, gg * D_STATE:(gg + 1) * D_STATE]
                        bt_scr[r0 // CHUNK + cc,
                               bcols + gg * D_STATE:bcols + (gg + 1) * D_STATE, :] = blk.T
            else:
                ccols = (j - c_tile0) * CONV_TILE
                c_scr[r0:r0 + ROW_TILE, ccols:ccols + CONV_TILE] = act

    def conv_steps():
        project_conv_tile(0)
        for j in range(n_conv_tiles):
            if j + 1 < n_conv_tiles:
                project_conv_tile(j + 1)
            conv_tile(j)
            yield

    def dt_steps():
        for d in range(2):
            dt_scr[d] = _softplus(_dot(hsrc[...], wdt_ref[d]) + dtb_ref[d:d + 1, :])
        yield

    if n_row_tiles == 1:
        for _ in itertools.zip_longest(itertools.chain(gmlp_steps(0), dt_steps()), conv_steps()):
            pass
    else:
        lax.fori_loop(0, n_row_tiles, gmlp_tile, 0)
        for _ in itertools.chain(conv_steps(), dt_steps()):
            pass

    lane = lax.broadcasted_iota(jnp.int32, (1, LANES), 1)
    a_rows = [jnp.where(lane < N_HEADS, -jnp.exp(alog_ref[d:d + 1, :]) * LOG2_E, 0.0)
              for d in range(2)]

    row_i = lax.broadcasted_iota(jnp.int32, (CHUNK, CHUNK), 0)
    col_i = lax.broadcasted_iota(jnp.int32, (CHUNK, CHUNK), 1)
    causal = [row_i >= col_i, row_i <= col_i]
    tri = [jnp.where(m, 1.0, 0.0).astype(BF16) for m in causal]
    glane = lax.broadcasted_iota(jnp.int32, (CHUNK, GROUP_W), 1)
    head_lanes = [(glane >= hh * HEAD_DIM) & (glane < (hh + 1) * HEAD_DIM)
                  for hh in range(HEADS_PER_GROUP)]
    expand = expand_ref[...]

    tables = []
    for d in range(2):
        for c in range(nc):
            rows = slice(c * CHUNK, (c + 1) * CHUNK)
            dt_c = dt_scr[d, rows, :]
            tables.append((d, c, rows, dt_c, _dot_exact_lhs(tri[d], dt_c * a_rows[d], 3)))
    for d, c, rows, dt_c, cum in tables:
        edge = CHUNK - 1 if d == 0 else 0
        edge_slab = slice(CHUNK - SUBLANES, CHUNK) if d == 0 else slice(0, SUBLANES)
        cum_scr[d, rows, :] = cum
        srct_scr[d, c] = (cum.T - jnp.log2(dt_c.T))[0:N_HEADS, :]
        wt_scr[d, c] = (dt_c * jnp.exp2(cum[edge:edge + 1, :] - cum)).T[0:N_HEADS, :]
        sdec_scr[d, c] = _dot_exact_rhs(jnp.exp2(cum[edge_slab, :]), expand, 2)

    def ssd_chunk(i, d):
        c = i if d == 0 else nc - 1 - i
        rows = pl.ds(pl.multiple_of(c * CHUNK, CHUNK), CHUNK)
        edge = CHUNK - 1 if d == 0 else 0
        cum = cum_scr[d, rows, :]
        src_t = srct_scr[d, c]
        w_t = wt_scr[d, c]
        st_decay = sdec_scr[d, c, edge % SUBLANES:edge % SUBLANES + 1, :]
        xs_c = xs_scr[rows, :]
        xs_b = xs_c.astype(BF16)
        y_parts, st_parts = [], []
        for g in range(N_GROUPS):
            cs = slice(g * GROUP_W, (g + 1) * GROUP_W)
            ns = slice(g * D_STATE, (g + 1) * D_STATE)
            c_g = c_scr[rows, ns]
            bt_g = bt_scr[c, ns, :]
            cb = _dot(c_g.astype(BF16), bt_g.astype(BF16))
            st_g = st_scr[d, :, cs]
            st_b = st_g.astype(BF16)
            x_g = xs_b[:, cs]
            zero = jnp.zeros_like(x_g)
            lhs_y, rhs_y, lhs_s, rhs_s = [], [], [], []
            for hh in range(HEADS_PER_GROUP):
                h = g * HEADS_PER_GROUP + hh
                cum_l = jnp.broadcast_to(cum[:, h:h + 1], (CHUNK, CHUNK))
                seg = jnp.where(causal[d], cum_l - src_t[h:h + 1, :], -jnp.inf)
                x_h = jnp.where(head_lanes[hh], x_g, zero)
                lhs_y.append((cb * jnp.exp2(seg)).astype(BF16))
                rhs_y.append(x_h)
                lhs_y.append((c_g * jnp.exp2(cum_l)).astype(BF16))
                rhs_y.append(jnp.where(head_lanes[hh], st_b, zero))
                lhs_s.append((bt_g * w_t[h:h + 1, :]).astype(BF16))
                rhs_s.append(x_h)
            y_parts.append(_dot(jnp.concatenate(lhs_y, axis=1), jnp.concatenate(rhs_y, axis=0)))
            st_parts.append(st_g * st_decay[:, cs] + _dot(jnp.concatenate(lhs_s, axis=1),
                                                          jnp.concatenate(rhs_s, axis=0)))
        return rows, y_parts, st_parts

    def ssd_step(i, carry):
        results = [ssd_chunk(i, d) for d in range(2)]
        for d, (rows, y_parts, st_parts) in enumerate(results):
            for g in range(N_GROUPS):
                st_scr[d, :, g * GROUP_W:(g + 1) * GROUP_W] = st_parts[g]
                for k in range(GROUP_W // LANES):
                    kt = g * (GROUP_W // LANES) + k
                    y_scr[kt, rows, :] = y_scr[kt, rows, :] + y_parts[g][:, k * LANES:(k + 1) * LANES]
        return carry

    for d in range(2):
        for k in range(D_SSM // D_STATE):
            ks = slice(k * D_STATE, (k + 1) * D_STATE)
            if has_h0:
                st_scr[d, :, ks] = h0_ref[0, 0, d, ks, :].T
            else:
                st_scr[d, :, ks] = jnp.zeros((D_STATE, D_STATE), F32)
    lax.fori_loop(0, nc, ssd_step, 0)
    if want_state:
        for d in range(2):
            for k in range(D_SSM // D_STATE):
                ks = slice(k * D_STATE, (k + 1) * D_STATE)
                st_ref[0, 0, d, ks, :] = st_scr[d, :, ks].T

    n_lane_tiles = D_SSM // LANES
    if col_major:
        rows_per_col = L // GRID_W
        for r in range(rows_per_col):
            for kt in range(n_lane_tiles):
                xs_scr[r * GRID_W:(r + 1) * GRID_W, kt * LANES:(kt + 1) * LANES] = (
                    y_scr[kt, pl.ds(r, GRID_W, stride=rows_per_col), :])

    def tail(t, carry):
        rows = row_tile(t)
        hn = h_scr[rows, :]
        if col_major:
            y_t = xs_scr[rows, :]
        else:
            y_t = jnp.concatenate([y_scr[kt, rows, :] for kt in range(n_lane_tiles)], axis=1)
        yz = y_t * _silu(_dot(hn, wzm_ref[...]))
        ym = (yz * lax.rsqrt(_mean_last(yz * yz) + EPS) * ssmw_ref[...]).astype(BF16)
        merged = (_sigmoid(_dot(hn, wglm_ref[...])) * _dot(ym, woutm_ref[...])
                  + out_ref[0, rows, :])
        o = x_ref[0, rows, :] + gate * _dot(merged.astype(BF16), wout_ref[...])
        if final_norm:
            o = o * lax.rsqrt(_mean_last(o * o) + EPS) * finw_ref[...]
        out_ref[0, rows, :] = o
        return carry

    over_row_tiles(tail)


def _param_spec(arr, layer):
    nd = arr.ndim
    if layer is None:
        return pl.BlockSpec(arr.shape, lambda b: (0,) * nd, pipeline_mode=pl.Buffered(1))
    return pl.BlockSpec((1,) + arr.shape[1:], lambda b: (layer,) + (0,) * (nd - 1),
                        pipeline_mode=pl.Buffered(1))


def _layer_call(x, mod, h0, weights, *, layer, state_buf, col_major, want_state, final_norm,
                mod_per_seq):
    nb, L, _ = x.shape
    nc = L // CHUNK
    has_h0 = h0 is not None
    state_aliased = want_state and state_buf is not None
    kernel = functools.partial(_layer_kernel, seq_len=L, col_major=col_major, has_h0=has_h0,
                               want_state=want_state, final_norm=final_norm,
                               state_aliased=state_aliased)
    single = pl.Buffered(1) if L * D_MODEL * 4 >= (4 << 20) else None
    state_block = (1, 1, 2, D_SSM, D_STATE)
    state_index = lambda b: (b, layer, 0, 0, 0)
    in_specs = [pl.BlockSpec((1, L, D_MODEL), lambda b: (b, 0, 0), pipeline_mode=single),
                pl.BlockSpec((1, 1, 3, D_MODEL), (lambda b: (layer, 1 + b, 0, 0)) if mod_per_seq
                             else (lambda b: (layer, 0, 0, 0)))]
    args = [x, mod]
    if has_h0:
        in_specs.append(pl.BlockSpec(state_block, state_index))
        args.append(h0)
    stacked, shared = weights
    for w in stacked:
        in_specs.append(_param_spec(w, layer))
        args.append(w)
    for w in shared:
        in_specs.append(_param_spec(w, None))
        args.append(w)
    aliases = {}
    if state_aliased:
        aliases[len(args)] = 1
        in_specs.append(pl.BlockSpec(memory_space=pl.ANY))
        args.append(state_buf)
    out_shape = [jax.ShapeDtypeStruct((nb, L, D_MODEL), F32)]
    out_specs = [pl.BlockSpec((1, L, D_MODEL), lambda b: (b, 0, 0), pipeline_mode=single)]
    if want_state:
        out_shape.append(jax.ShapeDtypeStruct((nb, DEPTH, 2, D_SSM, D_STATE), F32))
        out_specs.append(pl.BlockSpec(state_block, state_index))
    scratch = [
        pltpu.VMEM((L, D_MODEL), BF16),
        pltpu.VMEM((L if col_major else SUBLANES * 2, D_MODEL), BF16),
        pltpu.VMEM((2, L + 2 * CONV_PAD, CONV_TILE), F32),
        pltpu.VMEM((L, D_SSM), F32),
        pltpu.VMEM((L, N_GROUPS * D_STATE), F32),
        pltpu.VMEM((nc, N_GROUPS * D_STATE, CHUNK), F32),
        pltpu.VMEM((D_SSM // LANES, L, LANES), F32),
        pltpu.VMEM((2, D_STATE, D_SSM), F32),
        pltpu.VMEM((2, L, LANES), F32),
        pltpu.VMEM((ROW_TILE, D_GMLP), BF16),
        pltpu.VMEM((ROW_TILE, D_GMLP), F32),
        pltpu.VMEM((2, L, LANES), F32),
        pltpu.VMEM((2, nc, N_HEADS, CHUNK), F32),
        pltpu.VMEM((2, nc, N_HEADS, CHUNK), F32),
        pltpu.VMEM((2, nc, SUBLANES, D_SSM), F32),
    ]
    res = pl.pallas_call(
        kernel,
        grid=(nb,),
        in_specs=in_specs,
        out_specs=out_specs,
        out_shape=out_shape,
        scratch_shapes=scratch,
        input_output_aliases=aliases,
        compiler_params=pltpu.CompilerParams(
            dimension_semantics=("arbitrary",), vmem_limit_bytes=VMEM_LIMIT_BYTES),
        name="layer_L%d%s" % (L, "_cm" if col_major else ""),
    )(*args)
    return res if want_state else (res[0], None)


def _prep_weights(w_in, norm_w, conv_w, conv_b, dt_bias, a_log, d_skip, ssm_norm_w, w_out_m,
                  sgu_ln_w, sgu_ln_b, w_sp, b_sp, w_out_g, w_out, final_norm_w):
    w_a, w_b, w_m, w_g, w_o = _cast_weights(w_in, w_out_m, w_out_g, w_out)
    w_dt = w_in[:, :, DT_LO:DT_HI].astype(BF16).reshape(DEPTH, D_MODEL, 2, N_HEADS)
    w_dt = jnp.pad(w_dt.transpose(0, 2, 1, 3), ((0, 0), (0, 0), (0, 0), (0, LANES - N_HEADS)))
    pad_h = ((0, 0), (0, 0), (0, LANES - N_HEADS))
    rows = lambda v: v.reshape(DEPTH, 1, -1)
    expand = (jnp.arange(LANES)[:, None] == (jnp.arange(D_SSM)[None, :] // HEAD_DIM)).astype(BF16)
    stacked = [
        w_a, w_b, w_dt, w_m, w_g, w_o, w_sp.astype(BF16),
        rows(norm_w), conv_w, rows(conv_b),
        jnp.pad(dt_bias, pad_h), jnp.pad(a_log, pad_h),
        rows(jnp.repeat(d_skip, HEAD_DIM, axis=-1)), rows(ssm_norm_w),
        rows(sgu_ln_w), rows(sgu_ln_b),
        jnp.repeat(b_sp.transpose(0, 2, 1), D_GMLP // N_GROUPS_GMLP, axis=-1),
    ]
    return stacked, [final_norm_w.reshape(1, -1), expand]


def kernel(x_prompt, x_sample, state_ssm, c, c_ctx, w_ada, b_ada, norm_w, w_in, conv_w, conv_b,
           dt_bias, a_log, d_skip, ssm_norm_w, w_out_m, sgu_ln_w, sgu_ln_b, w_sp, b_sp, w_out_g,
           w_out, final_norm_w):
    n_lat = c.shape[0]
    n_ctx = x_prompt.shape[0]
    cc = jnp.concatenate([c_ctx[None, :], c,
                          jnp.zeros((SUBLANES - 1 - n_lat, D_MODEL), F32)], axis=0)
    mod = _modulation(cc, w_ada, b_ada).reshape(DEPTH, SUBLANES, 3, D_MODEL)
    h0_lat = state_ssm.reshape(n_lat, DEPTH, 2, D_SSM, D_STATE)
    weights = _prep_weights(w_in, norm_w, conv_w, conv_b, dt_bias, a_log, d_skip, ssm_norm_w,
                            w_out_m, sgu_ln_w, sgu_ln_b, w_sp, b_sp, w_out_g, w_out, final_norm_w)

    h_ctx, h_lat = x_prompt, x_sample
    states = None
    for i in range(DEPTH):
        last = i == DEPTH - 1
        h_ctx, states = _layer_call(h_ctx, mod, None, weights, layer=i, state_buf=states,
                                    col_major=False, want_state=True, final_norm=last,
                                    mod_per_seq=False)
        h_lat, _ = _layer_call(h_lat, mod, h0_lat, weights, layer=i,
                               state_buf=None, col_major=(i % 2 == 1), want_state=False,
                               final_norm=last, mod_per_seq=True)
    return h_ctx, h_lat, states.reshape(n_ctx, DEPTH, 2, N_HEADS, HEAD_DIM, D_STATE)
```

```python
import functools
import itertools

import jax
import jax.numpy as jnp
import numpy as np
from jax import lax
from jax.experimental import pallas as pl
from jax.experimental.pallas import tpu as pltpu

F32 = jnp.float32
BF16 = jnp.bfloat16

D_MODEL = 1024
DEPTH = 4
GRID_W = 64
D_SSM = 1024
HEAD_DIM = 64
N_HEADS = 16
N_GROUPS = 4
HEADS_PER_GROUP = N_HEADS // N_GROUPS
D_STATE = 128
D_CONV = 5
CHUNK = 128
CONV_DIM = D_SSM + 2 * N_GROUPS * D_STATE
D_GMLP = 1024
N_GROUPS_GMLP = 8
DT_LO = CONV_DIM + D_SSM
DT_HI = DT_LO + 2 * N_HEADS
_A_NAMES, _A_SIZES = ("xbc", "zm"), (CONV_DIM, D_SSM)
_B_NAMES, _B_SIZES = ("u", "v", "zg", "glm", "glg"), (D_GMLP, D_GMLP, D_GMLP, D_MODEL, D_MODEL)
_spans = lambda names, sizes: {n: (int(hi - sz), int(hi))
                               for n, hi, sz in zip(names, np.cumsum(sizes), sizes)}
W_A_SEGMENTS = _spans(_A_NAMES, _A_SIZES)
W_B_SEGMENTS = _spans(_B_NAMES, _B_SIZES)
EPS = 1e-6

LANES = 128
SUBLANES = 8
CONV_PAD = SUBLANES
CONV_TILE = 256
GROUP_W = HEADS_PER_GROUP * HEAD_DIM
ROW_TILE = 256
VMEM_LIMIT_BYTES = 60 * 1024 * 1024
LOG2_E = np.float32(1.4426950408889634)


def _dot(a, b):
    return jnp.dot(a, b, preferred_element_type=F32)


def _split_bf16(a, parts):
    out = []
    rem = a
    for i in range(parts):
        p = rem.astype(BF16)
        out.append(p)
        if i + 1 < parts:
            rem = rem - p.astype(F32)
    return out


def _dot_exact_rhs(a, m_bf16, parts):
    acc = None
    for p in _split_bf16(a, parts):
        t = _dot(p, m_bf16)
        acc = t if acc is None else acc + t
    return acc


def _dot_exact_lhs(m_bf16, a, parts):
    acc = None
    for p in _split_bf16(a, parts):
        t = _dot(m_bf16, p)
        acc = t if acc is None else acc + t
    return acc


def _sigmoid(x):
    return 1.0 / (1.0 + jnp.exp(-x))


def _silu(x):
    return x * _sigmoid(x)


def _gelu_tanh(x):
    c = np.float32(np.sqrt(2.0 / np.pi))
    return x * (0.5 * (1.0 + jnp.tanh(c * (x + 0.044715 * (x * x * x)))))


def _softplus(x):
    return jnp.maximum(x, 0.0) + jnp.log1p(jnp.exp(-jnp.abs(x)))


def _mean_last(x):
    return jnp.mean(x, axis=-1, keepdims=True)


def _mod_kernel(c_ref, w_ref, b_ref, o_ref):
    sc = _silu(c_ref[...]).astype(BF16)
    o_ref[0] = _dot(sc, w_ref[0].astype(BF16)) + b_ref[0]


def _modulation(cc, w_ada, b_ada):
    rows = cc.shape[0]
    ncol = 3 * D_MODEL // D_MODEL
    return pl.pallas_call(
        _mod_kernel,
        grid=(DEPTH, ncol),
        in_specs=[
            pl.BlockSpec((rows, D_MODEL), lambda i, j: (0, 0)),
            pl.BlockSpec((1, D_MODEL, D_MODEL), lambda i, j: (i, 0, j)),
            pl.BlockSpec((1, 1, D_MODEL), lambda i, j: (i, 0, j)),
        ],
        out_specs=pl.BlockSpec((1, rows, D_MODEL), lambda i, j: (i, 0, j)),
        out_shape=jax.ShapeDtypeStruct((DEPTH, rows, 3 * D_MODEL), F32),
        name="adaln_mod",
    )(cc, w_ada, b_ada.reshape(DEPTH, 1, 3 * D_MODEL))


def _layer_kernel(*refs, seq_len, col_major, has_h0, want_state, final_norm, state_aliased):
    L = seq_len
    nc = L // CHUNK
    it = iter(refs)
    x_ref = next(it)
    mod_ref = next(it)
    h0_ref = next(it) if has_h0 else None
    (wa_ref, wb_ref, wdt_ref, woutm_ref, woutg_ref, wout_ref, wsp_ref,
     normw_ref, convw_ref, convb_ref, dtb_ref, alog_ref, dskip_ref, ssmw_ref,
     lnw_ref, lnb_ref, bsp_ref, finw_ref, expand_ref) = (next(it) for _ in range(19))
    w_seg = {name: wa_ref.at[0, :, lo:hi] for name, (lo, hi) in W_A_SEGMENTS.items()}
    w_seg.update({name: wb_ref.at[0, :, lo:hi] for name, (lo, hi) in W_B_SEGMENTS.items()})
    wxbc_ref, wzm_ref, wu_ref, wv_ref = w_seg["xbc"], w_seg["zm"], w_seg["u"], w_seg["v"]
    wzg_ref, wglm_ref, wglg_ref = w_seg["zg"], w_seg["glm"], w_seg["glg"]
    wdt_ref, woutm_ref, woutg_ref, wout_ref, wsp_ref = (
        r.at[0] for r in (wdt_ref, woutm_ref, woutg_ref, wout_ref, wsp_ref))
    (normw_ref, convw_ref, convb_ref, dtb_ref, alog_ref, dskip_ref, ssmw_ref, lnw_ref, lnb_ref,
     bsp_ref) = (r.at[0] for r in (normw_ref, convw_ref, convb_ref, dtb_ref, alog_ref, dskip_ref,
                                   ssmw_ref, lnw_ref, lnb_ref, bsp_ref))
    if state_aliased:
        next(it)
    out_ref = next(it)
    st_ref = next(it) if want_state else None
    (h_scr, hcm_scr, xraw_scr, xs_scr, c_scr, bt_scr, y_scr, st_scr,
     dt_scr, v_scr, s_scr, cum_scr, srct_scr, wt_scr, sdec_scr) = (next(it) for _ in range(15))

    shift = mod_ref[0, 0, 0:1, :]
    scale = mod_ref[0, 0, 1:2, :]
    gate = mod_ref[0, 0, 2:3, :]
    normw = normw_ref[...]
    n_row_tiles = L // ROW_TILE

    def norm_mod(xt):
        y = xt * lax.rsqrt(_mean_last(xt * xt) + EPS) * normw
        return (y * (1.0 + scale) + shift).astype(BF16)

    def row_tile(t):
        if isinstance(t, int):
            return slice(t * ROW_TILE, (t + 1) * ROW_TILE)
        return pl.ds(pl.multiple_of(t * ROW_TILE, ROW_TILE), ROW_TILE)

    def over_row_tiles(body):
        if n_row_tiles == 1:
            body(0, 0)
        else:
            lax.fori_loop(0, n_row_tiles, body, 0)

    for r0 in range(0, L, ROW_TILE):
        h_scr[r0:r0 + ROW_TILE, :] = norm_mod(x_ref[0, r0:r0 + ROW_TILE, :])
    if col_major:
        rows_per_col = L // GRID_W
        for r0 in range(0, L, ROW_TILE):
            dst = r0 + lax.broadcasted_iota(jnp.int32, (ROW_TILE, L), 0)
            src = lax.broadcasted_iota(jnp.int32, (ROW_TILE, L), 1)
            want = (dst % rows_per_col) * GRID_W + dst // rows_per_col
            perm = jnp.where(src == want, 1.0, 0.0).astype(BF16)
            hcm_scr[r0:r0 + ROW_TILE, :] = _dot(perm, h_scr[...]).astype(BF16)
        hsrc = hcm_scr
    else:
        hsrc = h_scr

    def gmlp_steps(t):
        rows = row_tile(t)
        hn = h_scr[rows, :]
        v = _gelu_tanh(_dot(hn, wv_ref[...]))
        vc = v - _mean_last(v)
        v_scr[...] = (vc * lax.rsqrt(_mean_last(vc * vc) + EPS) * lnw_ref[...]
                      + lnb_ref[...]).astype(BF16)
        yield
        for cc in range(ROW_TILE // CHUNK):
            for g in range(N_GROUPS_GMLP):
                blk = (slice(cc * CHUNK, (cc + 1) * CHUNK), slice(g * CHUNK, (g + 1) * CHUNK))
                s_scr[blk] = _dot(wsp_ref[g], v_scr[blk]) + bsp_ref[:, blk[1]]
            yield
        u = _gelu_tanh(_dot(hn, wu_ref[...]))
        yield
        yg = (u * s_scr[...] * _silu(_dot(hn, wzg_ref[...]))).astype(BF16)
        yield
        og = _dot(yg, woutg_ref[...])
        yield
        out_ref[0, rows, :] = _sigmoid(_dot(hn, wglg_ref[...])) * og
        yield

    def gmlp_tile(t, carry):
        for _ in gmlp_steps(t):
            pass
        return carry

    zero_pad = jnp.zeros((CONV_PAD, CONV_TILE), F32)
    for buf in range(2):
        xraw_scr[buf, 0:CONV_PAD, :] = zero_pad
        xraw_scr[buf, CONV_PAD + L:CONV_PAD + L + CONV_PAD, :] = zero_pad
    half = D_CONV // 2
    n_conv_tiles = CONV_DIM // CONV_TILE
    b_tile0 = D_SSM // CONV_TILE
    c_tile0 = (D_SSM + N_GROUPS * D_STATE) // CONV_TILE

    def project_conv_tile(j):
        cols = slice(j * CONV_TILE, (j + 1) * CONV_TILE)
        xraw_scr[j % 2, CONV_PAD:CONV_PAD + L, :] = _dot(hsrc[...], wxbc_ref[:, cols])

    def conv_tile(j):
        cols = slice(j * CONV_TILE, (j + 1) * CONV_TILE)
        for r0 in range(0, L, ROW_TILE):
            acc = convb_ref[:, cols]
            for k in range(D_CONV):
                start = CONV_PAD + r0 + k - half
                acc = acc + convw_ref[k:k + 1, cols] * xraw_scr[j % 2, start:start + ROW_TILE, :]
            act = _silu(acc)
            if j < b_tile0:
                xs_scr[r0:r0 + ROW_TILE, cols] = act
                skip = dskip_ref[:, cols] * act
                for k in range(CONV_TILE // LANES):
                    y_scr[j * (CONV_TILE // LANES) + k, r0:r0 + ROW_TILE, :] = (
                        skip[:, k * LANES:(k + 1) * LANES])
            elif j < c_tile0:
                bcols = (j - b_tile0) * CONV_TILE
                for cc in range(ROW_TILE // CHUNK):
                    for gg in range(CONV_TILE // D_STATE):
                        blk = act[cc * CHUNK:(cc + 1) * CHUNK, gg * D_STATE:(gg + 1) * D_STATE]
                        bt_scr[r0 // CHUNK + cc,
                               bcols + gg * D_STATE:bcols + (gg + 1) * D_STATE, :] = blk.T
            else:
                ccols = (j - c_tile0) * CONV_TILE
                c_scr[r0:r0 + ROW_TILE, ccols:ccols + CONV_TILE] = act

    def conv_steps():
        project_conv_tile(0)
        for j in range(n_conv_tiles):
            if j + 1 < n_conv_tiles:
                project_conv_tile(j + 1)
            conv_tile(j)
            yield

    def dt_steps():
        for d in range(2):
            dt_scr[d] = _softplus(_dot(hsrc[...], wdt_ref[d]) + dtb_ref[d:d + 1, :])
        yield

    if n_row_tiles == 1:
        for _ in itertools.zip_longest(itertools.chain(gmlp_steps(0), dt_steps()), conv_steps()):
            pass
    else:
        lax.fori_loop(0, n_row_tiles, gmlp_tile, 0)
        for _ in itertools.chain(conv_steps(), dt_steps()):
            pass

    lane = lax.broadcasted_iota(jnp.int32, (1, LANES), 1)
    a_rows = [jnp.where(lane < N_HEADS, -jnp.exp(alog_ref[d:d + 1, :]) * LOG2_E, 0.0)
              for d in range(2)]

    row_i = lax.broadcasted_iota(jnp.int32, (CHUNK, CHUNK), 0)
    col_i = lax.broadcasted_iota(jnp.int32, (CHUNK, CHUNK), 1)
    causal = [row_i >= col_i, row_i <= col_i]
    tri = [jnp.where(m, 1.0, 0.0).astype(BF16) for m in causal]
    glane = lax.broadcasted_iota(jnp.int32, (CHUNK, GROUP_W), 1)
    head_lanes = [(glane >= hh * HEAD_DIM) & (glane < (hh + 1) * HEAD_DIM)
                  for hh in range(HEADS_PER_GROUP)]
    expand = expand_ref[...]

    tables = []
    for d in range(2):
        for c in range(nc):
            rows = slice(c * CHUNK, (c + 1) * CHUNK)
            dt_c = dt_scr[d, rows, :]
            tables.append((d, c, rows, dt_c, _dot_exact_lhs(tri[d], dt_c * a_rows[d], 3)))
    for d, c, rows, dt_c, cum in tables:
        edge = CHUNK - 1 if d == 0 else 0
        edge_slab = slice(CHUNK - SUBLANES, CHUNK) if d == 0 else slice(0, SUBLANES)
        cum_scr[d, rows, :] = cum
        srct_scr[d, c] = (cum.T - jnp.log2(dt_c.T))[0:N_HEADS, :]
        wt_scr[d, c] = (dt_c * jnp.exp2(cum[edge:edge + 1, :] - cum)).T[0:N_HEADS, :]
        sdec_scr[d, c] = _dot_exact_rhs(jnp.exp2(cum[edge_slab, :]), expand, 2)

    def ssd_chunk(i, d, zero_state):
        c = i if d == 0 else nc - 1 - i
        rows = (slice(c * CHUNK, (c + 1) * CHUNK) if isinstance(c, int)
                else pl.ds(pl.multiple_of(c * CHUNK, CHUNK), CHUNK))
        edge = CHUNK - 1 if d == 0 else 0
        cum = cum_scr[d, rows, :]
        src_t = srct_scr[d, c]
        w_t = wt_scr[d, c]
        st_decay = sdec_scr[d, c, edge % SUBLANES:edge % SUBLANES + 1, :]
        xs_c = xs_scr[rows, :]
        xs_b = xs_c.astype(BF16)
        y_parts, st_parts = [], []
        for g in range(N_GROUPS):
            cs = slice(g * GROUP_W, (g + 1) * GROUP_W)
            ns = slice(g * D_STATE, (g + 1) * D_STATE)
            c_g = c_scr[rows, ns]
            bt_g = bt_scr[c, ns, :]
            cb = _dot(c_g.astype(BF16), bt_g.astype(BF16))
            if not zero_state:
                st_g = st_scr[d, :, cs]
                st_b = st_g.astype(BF16)
            x_g = xs_b[:, cs]
            zero = jnp.zeros_like(x_g)
            lhs_y, rhs_y, lhs_s, rhs_s = [], [], [], []
            for hh in range(HEADS_PER_GROUP):
                h = g * HEADS_PER_GROUP + hh
                cum_l = jnp.broadcast_to(cum[:, h:h + 1], (CHUNK, CHUNK))
                seg = jnp.where(causal[d], cum_l - src_t[h:h + 1, :], -jnp.inf)
                x_h = jnp.where(head_lanes[hh], x_g, zero)
                lhs_y.append((cb * jnp.exp2(seg)).astype(BF16))
                rhs_y.append(x_h)
                if not zero_state:
                    lhs_y.append((c_g * jnp.exp2(cum_l)).astype(BF16))
                    rhs_y.append(jnp.where(head_lanes[hh], st_b, zero))
                lhs_s.append((bt_g * w_t[h:h + 1, :]).astype(BF16))
                rhs_s.append(x_h)
            y_parts.append(_dot(jnp.concatenate(lhs_y, axis=1), jnp.concatenate(rhs_y, axis=0)))
            st_new = _dot(jnp.concatenate(lhs_s, axis=1), jnp.concatenate(rhs_s, axis=0))
            st_parts.append(st_new if zero_state else st_g * st_decay[:, cs] + st_new)
        return rows, y_parts, st_parts

    def ssd_step(i, carry, zero_state=False):
        results = [ssd_chunk(i, d, zero_state) for d in range(2)]
        for d, (rows, y_parts, st_parts) in enumerate(results):
            for g in range(N_GROUPS):
                st_scr[d, :, g * GROUP_W:(g + 1) * GROUP_W] = st_parts[g]
                for k in range(GROUP_W // LANES):
                    kt = g * (GROUP_W // LANES) + k
                    y_scr[kt, rows, :] = y_scr[kt, rows, :] + y_parts[g][:, k * LANES:(k + 1) * LANES]
        return carry

    if has_h0:
        for d in range(2):
            for k in range(D_SSM // D_STATE):
                ks = slice(k * D_STATE, (k + 1) * D_STATE)
                st_scr[d, :, ks] = h0_ref[0, 0, d, ks, :].T
        lax.fori_loop(0, nc, ssd_step, 0, unroll=2)
    else:
        ssd_step(0, 0, zero_state=True)
        lax.fori_loop(1, nc, ssd_step, 0)
    if want_state:
        for d in range(2):
            for k in range(D_SSM // D_STATE):
                ks = slice(k * D_STATE, (k + 1) * D_STATE)
                st_ref[0, 0, d, ks, :] = st_scr[d, :, ks].T

    n_lane_tiles = D_SSM // LANES
    if col_major:
        rows_per_col = L // GRID_W
        for r in range(rows_per_col):
            for kt in range(n_lane_tiles):
                xs_scr[r * GRID_W:(r + 1) * GRID_W, kt * LANES:(kt + 1) * LANES] = (
                    y_scr[kt, pl.ds(r, GRID_W, stride=rows_per_col), :])

    def tail(t, carry):
        rows = row_tile(t)
        hn = h_scr[rows, :]
        if col_major:
            y_t = xs_scr[rows, :]
        else:
            y_t = jnp.concatenate([y_scr[kt, rows, :] for kt in range(n_lane_tiles)], axis=1)
        yz = y_t * _silu(_dot(hn, wzm_ref[...]))
        ym = (yz * lax.rsqrt(_mean_last(yz * yz) + EPS) * ssmw_ref[...]).astype(BF16)
        merged = (_sigmoid(_dot(hn, wglm_ref[...])) * _dot(ym, woutm_ref[...])
                  + out_ref[0, rows, :])
        o = x_ref[0, rows, :] + gate * _dot(merged.astype(BF16), wout_ref[...])
        if final_norm:
            o = o * lax.rsqrt(_mean_last(o * o) + EPS) * finw_ref[...]
        out_ref[0, rows, :] = o
        return carry

    over_row_tiles(tail)


def _param_spec(arr, layer):
    nd = arr.ndim
    if layer is None:
        return pl.BlockSpec(arr.shape, lambda b: (0,) * nd, pipeline_mode=pl.Buffered(1))
    return pl.BlockSpec((1,) + arr.shape[1:], lambda b: (layer,) + (0,) * (nd - 1),
                        pipeline_mode=pl.Buffered(1))


def _layer_call(x, mod, h0, weights, *, layer, state_buf, col_major, want_state, final_norm,
                mod_per_seq):
    nb, L, _ = x.shape
    nc = L // CHUNK
    has_h0 = h0 is not None
    state_aliased = want_state and state_buf is not None
    kernel = functools.partial(_layer_kernel, seq_len=L, col_major=col_major, has_h0=has_h0,
                               want_state=want_state, final_norm=final_norm,
                               state_aliased=state_aliased)
    single = pl.Buffered(1) if L * D_MODEL * 4 >= (4 << 20) else None
    state_block = (1, 1, 2, D_SSM, D_STATE)
    state_index = lambda b: (b, layer, 0, 0, 0)
    in_specs = [pl.BlockSpec((1, L, D_MODEL), lambda b: (b, 0, 0), pipeline_mode=single),
                pl.BlockSpec((1, 1, 3, D_MODEL), (lambda b: (layer, 1 + b, 0, 0)) if mod_per_seq
                             else (lambda b: (layer, 0, 0, 0)))]
    args = [x, mod]
    if has_h0:
        in_specs.append(pl.BlockSpec(state_block, state_index))
        args.append(h0)
    stacked, shared = weights
    for w in stacked:
        in_specs.append(_param_spec(w, layer))
        args.append(w)
    for w in shared:
        in_specs.append(_param_spec(w, None))
        args.append(w)
    aliases = {}
    if state_aliased:
        aliases[len(args)] = 1
        in_specs.append(pl.BlockSpec(memory_space=pl.ANY))
        args.append(state_buf)
    out_shape = [jax.ShapeDtypeStruct((nb, L, D_MODEL), F32)]
    out_specs = [pl.BlockSpec((1, L, D_MODEL), lambda b: (b, 0, 0), pipeline_mode=single)]
    if want_state:
        out_shape.append(jax.ShapeDtypeStruct((nb, DEPTH, 2, D_SSM, D_STATE), F32))
        out_specs.append(pl.BlockSpec(state_block, state_index))
    scratch = [
        pltpu.VMEM((L, D_MODEL), BF16),
        pltpu.VMEM((L if col_major else SUBLANES * 2, D_MODEL), BF16),
        pltpu.VMEM((2, L + 2 * CONV_PAD, CONV_TILE), F32),
        pltpu.VMEM((L, D_SSM), F32),
        pltpu.VMEM((L, N_GROUPS * D_STATE), F32),
        pltpu.VMEM((nc, N_GROUPS * D_STATE, CHUNK), F32),
        pltpu.VMEM((D_SSM // LANES, L, LANES), F32),
        pltpu.VMEM((2, D_STATE, D_SSM), F32),
        pltpu.VMEM((2, L, LANES), F32),
        pltpu.VMEM((ROW_TILE, D_GMLP), BF16),
        pltpu.VMEM((ROW_TILE, D_GMLP), F32),
        pltpu.VMEM((2, L, LANES), F32),
        pltpu.VMEM((2, nc, N_HEADS, CHUNK), F32),
        pltpu.VMEM((2, nc, N_HEADS, CHUNK), F32),
        pltpu.VMEM((2, nc, SUBLANES, D_SSM), F32),
    ]
    res = pl.pallas_call(
        kernel,
        grid=(nb,),
        in_specs=in_specs,
        out_specs=out_specs,
        out_shape=out_shape,
        scratch_shapes=scratch,
        input_output_aliases=aliases,
        compiler_params=pltpu.CompilerParams(
            dimension_semantics=("arbitrary",), vmem_limit_bytes=VMEM_LIMIT_BYTES),
        name="layer_L%d%s" % (L, "_cm" if col_major else ""),
    )(*args)
    return res if want_state else (res[0], None)


def _prep_weights(w_in, norm_w, conv_w, conv_b, dt_bias, a_log, d_skip, ssm_norm_w, w_out_m,
                  sgu_ln_w, sgu_ln_b, w_sp, b_sp, w_out_g, w_out, final_norm_w):
    w_a = w_in[:, :, :DT_LO].astype(BF16)
    w_b = w_in[:, :, DT_HI:].astype(BF16)
    w_m, w_g, w_o = w_out_m.astype(BF16), w_out_g.astype(BF16), w_out.astype(BF16)
    w_dt = w_in[:, :, DT_LO:DT_HI].astype(BF16).reshape(DEPTH, D_MODEL, 2, N_HEADS)
    w_dt = jnp.pad(w_dt.transpose(0, 2, 1, 3), ((0, 0), (0, 0), (0, 0), (0, LANES - N_HEADS)))
    pad_h = ((0, 0), (0, 0), (0, LANES - N_HEADS))
    rows = lambda v: v.reshape(DEPTH, 1, -1)
    expand = (jnp.arange(LANES)[:, None] == (jnp.arange(D_SSM)[None, :] // HEAD_DIM)).astype(BF16)
    stacked = [
        w_a, w_b, w_dt, w_m, w_g, w_o, w_sp.astype(BF16),
        rows(norm_w), conv_w, rows(conv_b),
        jnp.pad(dt_bias, pad_h), jnp.pad(a_log, pad_h),
        rows(jnp.repeat(d_skip, HEAD_DIM, axis=-1)), rows(ssm_norm_w),
        rows(sgu_ln_w), rows(sgu_ln_b),
        jnp.repeat(b_sp.transpose(0, 2, 1), D_GMLP // N_GROUPS_GMLP, axis=-1),
    ]
    return stacked, [final_norm_w.reshape(1, -1), expand]


def kernel(x_prompt, x_sample, state_ssm, c, c_ctx, w_ada, b_ada, norm_w, w_in, conv_w, conv_b,
           dt_bias, a_log, d_skip, ssm_norm_w, w_out_m, sgu_ln_w, sgu_ln_b, w_sp, b_sp, w_out_g,
           w_out, final_norm_w):
    n_lat = c.shape[0]
    n_ctx = x_prompt.shape[0]
    cc = jnp.concatenate([c_ctx[None, :], c,
                          jnp.zeros((SUBLANES - 1 - n_lat, D_MODEL), F32)], axis=0)
    mod = _modulation(cc, w_ada, b_ada).reshape(DEPTH, SUBLANES, 3, D_MODEL)
    h0_lat = state_ssm.reshape(n_lat, DEPTH, 2, D_SSM, D_STATE)
    weights = _prep_weights(w_in, norm_w, conv_w, conv_b, dt_bias, a_log, d_skip, ssm_norm_w,
                            w_out_m, sgu_ln_w, sgu_ln_b, w_sp, b_sp, w_out_g, w_out, final_norm_w)

    h_ctx, h_lat = x_prompt, x_sample
    states = None
    for i in range(DEPTH):
        last = i == DEPTH - 1
        h_ctx, states = _layer_call(h_ctx, mod, None, weights, layer=i, state_buf=states,
                                    col_major=False, want_state=True, final_norm=last,
                                    mod_per_seq=False)
        h_lat, _ = _layer_call(h_lat, mod, h0_lat, weights, layer=i,
                               state_buf=None, col_major=(i % 2 == 1), want_state=False,
                               final_norm=last, mod_per_seq=True)
    return h_ctx, h_lat, states.reshape(n_ctx, DEPTH, 2, N_HEADS, HEAD_DIM, D_STATE)
```

```python
import functools
import itertools

import jax
import jax.numpy as jnp
import numpy as np
from jax import lax
from jax.experimental import pallas as pl
from jax.experimental.pallas import tpu as pltpu

F32 = jnp.float32
BF16 = jnp.bfloat16

D_MODEL = 1024
DEPTH = 4
GRID_W = 64
D_SSM = 1024
HEAD_DIM = 64
N_HEADS = 16
N_GROUPS = 4
HEADS_PER_GROUP = N_HEADS // N_GROUPS
D_STATE = 128
D_CONV = 5
CHUNK = 128
CONV_DIM = D_SSM + 2 * N_GROUPS * D_STATE
D_GMLP = 1024
N_GROUPS_GMLP = 8
DT_LO = CONV_DIM + D_SSM
DT_HI = DT_LO + 2 * N_HEADS
_A_NAMES, _A_SIZES = ("xbc", "zm"), (CONV_DIM, D_SSM)
_B_NAMES, _B_SIZES = ("u", "v", "zg", "glm", "glg"), (D_GMLP, D_GMLP, D_GMLP, D_MODEL, D_MODEL)
_spans = lambda names, sizes: {n: (int(hi - sz), int(hi))
                               for n, hi, sz in zip(names, np.cumsum(sizes), sizes)}
W_A_SEGMENTS = _spans(_A_NAMES, _A_SIZES)
W_B_SEGMENTS = _spans(_B_NAMES, _B_SIZES)
EPS = 1e-6

LANES = 128
SUBLANES = 8
CONV_PAD = SUBLANES
CONV_TILE = 256
GROUP_W = HEADS_PER_GROUP * HEAD_DIM
ROW_TILE = 256
VMEM_LIMIT_BYTES = 60 * 1024 * 1024
LOG2_E = np.float32(1.4426950408889634)


def _dot(a, b):
    return jnp.dot(a, b, preferred_element_type=F32)


def _split_bf16(a, parts):
    out = []
    rem = a
    for i in range(parts):
        p = rem.astype(BF16)
        out.append(p)
        if i + 1 < parts:
            rem = rem - p.astype(F32)
    return out


def _dot_exact_rhs(a, m_bf16, parts):
    acc = None
    for p in _split_bf16(a, parts):
        t = _dot(p, m_bf16)
        acc = t if acc is None else acc + t
    return acc


def _dot_exact_lhs(m_bf16, a, parts):
    acc = None
    for p in _split_bf16(a, parts):
        t = _dot(m_bf16, p)
        acc = t if acc is None else acc + t
    return acc


def _sigmoid(x):
    return 1.0 / (1.0 + jnp.exp(-x))


def _silu(x):
    return x * _sigmoid(x)


def _gelu_tanh(x):
    c = np.float32(np.sqrt(2.0 / np.pi))
    return x * (0.5 * (1.0 + jnp.tanh(c * (x + 0.044715 * (x * x * x)))))


def _softplus(x):
    return jnp.maximum(x, 0.0) + jnp.log1p(jnp.exp(-jnp.abs(x)))


def _mean_last(x):
    return jnp.mean(x, axis=-1, keepdims=True)


def _mod_kernel(c_ref, w_ref, b_ref, o_ref):
    sc = _silu(c_ref[...]).astype(BF16)
    o_ref[0] = _dot(sc, w_ref[0].astype(BF16)) + b_ref[0]


def _modulation(cc, w_ada, b_ada):
    rows = cc.shape[0]
    ncol = 3 * D_MODEL // D_MODEL
    return pl.pallas_call(
        _mod_kernel,
        grid=(DEPTH, ncol),
        in_specs=[
            pl.BlockSpec((rows, D_MODEL), lambda i, j: (0, 0)),
            pl.BlockSpec((1, D_MODEL, D_MODEL), lambda i, j: (i, 0, j)),
            pl.BlockSpec((1, 1, D_MODEL), lambda i, j: (i, 0, j)),
        ],
        out_specs=pl.BlockSpec((1, rows, D_MODEL), lambda i, j: (i, 0, j)),
        out_shape=jax.ShapeDtypeStruct((DEPTH, rows, 3 * D_MODEL), F32),
        name="adaln_mod",
    )(cc, w_ada, b_ada.reshape(DEPTH, 1, 3 * D_MODEL))


def _layer_kernel(*refs, seq_len, col_major, has_h0, want_state, final_norm, state_aliased):
    L = seq_len
    nc = L // CHUNK
    it = iter(refs)
    x_ref = next(it)
    mod_ref = next(it)
    h0_ref = next(it) if has_h0 else None
    (wa_ref, wb_ref, wdt_ref, woutm_ref, woutg_ref, wout_ref, wsp_ref,
     normw_ref, convw_ref, convb_ref, dtb_ref, alog_ref, dskip_ref, ssmw_ref,
     lnw_ref, lnb_ref, bsp_ref, finw_ref, expand_ref) = (next(it) for _ in range(19))
    w_seg = {name: wa_ref.at[:, lo:hi] for name, (lo, hi) in W_A_SEGMENTS.items()}
    w_seg.update({name: wb_ref.at[:, lo:hi] for name, (lo, hi) in W_B_SEGMENTS.items()})
    wxbc_ref, wzm_ref, wu_ref, wv_ref = w_seg["xbc"], w_seg["zm"], w_seg["u"], w_seg["v"]
    wzg_ref, wglm_ref, wglg_ref = w_seg["zg"], w_seg["glm"], w_seg["glg"]
    wdt_ref, woutm_ref, woutg_ref, wout_ref, wsp_ref = (
        r.at[0] for r in (wdt_ref, woutm_ref, woutg_ref, wout_ref, wsp_ref))
    (normw_ref, convw_ref, convb_ref, dtb_ref, alog_ref, dskip_ref, ssmw_ref, lnw_ref, lnb_ref,
     bsp_ref) = (r.at[0] for r in (normw_ref, convw_ref, convb_ref, dtb_ref, alog_ref, dskip_ref,
                                   ssmw_ref, lnw_ref, lnb_ref, bsp_ref))
    if state_aliased:
        next(it)
    out_ref = next(it)
    st_ref = next(it) if want_state else None
    (h_scr, hcm_scr, xraw_scr, xs_scr, c_scr, bt_scr, y_scr, st_scr,
     dt_scr, v_scr, s_scr, cum_scr, srct_scr, wt_scr, sdec_scr) = (next(it) for _ in range(15))

    shift = mod_ref[0, 0, 0:1, :]
    scale = mod_ref[0, 0, 1:2, :]
    gate = mod_ref[0, 0, 2:3, :]
    normw = normw_ref[...]
    n_row_tiles = L // ROW_TILE

    def norm_mod(xt):
        y = xt * lax.rsqrt(_mean_last(xt * xt) + EPS) * normw
        return (y * (1.0 + scale) + shift).astype(BF16)

    def row_tile(t):
        if isinstance(t, int):
            return slice(t * ROW_TILE, (t + 1) * ROW_TILE)
        return pl.ds(pl.multiple_of(t * ROW_TILE, ROW_TILE), ROW_TILE)

    def over_row_tiles(body):
        if n_row_tiles == 1:
            body(0, 0)
        else:
            lax.fori_loop(0, n_row_tiles, body, 0)

    for r0 in range(0, L, ROW_TILE):
        h_scr[r0:r0 + ROW_TILE, :] = norm_mod(x_ref[0, r0:r0 + ROW_TILE, :])
    if col_major:
        rows_per_col = L // GRID_W
        for r0 in range(0, L, ROW_TILE):
            dst = r0 + lax.broadcasted_iota(jnp.int32, (ROW_TILE, L), 0)
            src = lax.broadcasted_iota(jnp.int32, (ROW_TILE, L), 1)
            want = (dst % rows_per_col) * GRID_W + dst // rows_per_col
            perm = jnp.where(src == want, 1.0, 0.0).astype(BF16)
            hcm_scr[r0:r0 + ROW_TILE, :] = _dot(perm, h_scr[...]).astype(BF16)
        hsrc = hcm_scr
    else:
        hsrc = h_scr

    def gmlp_steps(t):
        rows = row_tile(t)
        hn = h_scr[rows, :]
        v = _gelu_tanh(_dot(hn, wv_ref[...]))
        vc = v - _mean_last(v)
        v_scr[...] = (vc * lax.rsqrt(_mean_last(vc * vc) + EPS) * lnw_ref[...]
                      + lnb_ref[...]).astype(BF16)
        yield
        for cc in range(ROW_TILE // CHUNK):
            for g in range(N_GROUPS_GMLP):
                blk = (slice(cc * CHUNK, (cc + 1) * CHUNK), slice(g * CHUNK, (g + 1) * CHUNK))
                s_scr[blk] = _dot(wsp_ref[g], v_scr[blk]) + bsp_ref[:, blk[1]]
            yield
        u = _gelu_tanh(_dot(hn, wu_ref[...]))
        yield
        yg = (u * s_scr[...] * _silu(_dot(hn, wzg_ref[...]))).astype(BF16)
        yield
        og = _dot(yg, woutg_ref[...])
        yield
        out_ref[0, rows, :] = _sigmoid(_dot(hn, wglg_ref[...])) * og
        yield

    def gmlp_tile(t, carry):
        for _ in gmlp_steps(t):
            pass
        return carry

    zero_pad = jnp.zeros((CONV_PAD, CONV_TILE), F32)
    for buf in range(2):
        xraw_scr[buf, 0:CONV_PAD, :] = zero_pad
        xraw_scr[buf, CONV_PAD + L:CONV_PAD + L + CONV_PAD, :] = zero_pad
    half = D_CONV // 2
    n_conv_tiles = CONV_DIM // CONV_TILE
    b_tile0 = D_SSM // CONV_TILE
    c_tile0 = (D_SSM + N_GROUPS * D_STATE) // CONV_TILE

    def project_conv_tile(j):
        cols = slice(j * CONV_TILE, (j + 1) * CONV_TILE)
        xraw_scr[j % 2, CONV_PAD:CONV_PAD + L, :] = _dot(hsrc[...], wxbc_ref[:, cols])

    def conv_tile(j):
        cols = slice(j * CONV_TILE, (j + 1) * CONV_TILE)
        for r0 in range(0, L, ROW_TILE):
            acc = convb_ref[:, cols]
            for k in range(D_CONV):
                start = CONV_PAD + r0 + k - half
                acc = acc + convw_ref[k:k + 1, cols] * xraw_scr[j % 2, start:start + ROW_TILE, :]
            act = _silu(acc)
            if j < b_tile0:
                xs_scr[r0:r0 + ROW_TILE, cols] = act
                skip = dskip_ref[:, cols] * act
                for k in range(CONV_TILE // LANES):
                    y_scr[j * (CONV_TILE // LANES) + k, r0:r0 + ROW_TILE, :] = (
                        skip[:, k * LANES:(k + 1) * LANES])
            elif j < c_tile0:
                bcols = (j - b_tile0) * CONV_TILE
                for cc in range(ROW_TILE // CHUNK):
                    for gg in range(CONV_TILE // D_STATE):
                        blk = act[cc * CHUNK:(cc + 1) * CHUNK, gg * D_STATE:(gg + 1) * D_STATE]
                        bt_scr[r0 // CHUNK + cc,
                               bcols + gg * D_STATE:bcols + (gg + 1) * D_STATE, :] = blk.T
            else:
                ccols = (j - c_tile0) * CONV_TILE
                c_scr[r0:r0 + ROW_TILE, ccols:ccols + CONV_TILE] = act

    def conv_steps():
        project_conv_tile(0)
        for j in range(n_conv_tiles):
            if j + 1 < n_conv_tiles:
                project_conv_tile(j + 1)
            conv_tile(j)
            yield

    def dt_steps():
        for d in range(2):
            dt_scr[d] = _softplus(_dot(hsrc[...], wdt_ref[d]) + dtb_ref[d:d + 1, :])
        yield

    if n_row_tiles == 1:
        for _ in itertools.zip_longest(itertools.chain(gmlp_steps(0), dt_steps()), conv_steps()):
            pass
    else:
        lax.fori_loop(0, n_row_tiles, gmlp_tile, 0)
        for _ in itertools.chain(conv_steps(), dt_steps()):
            pass

    lane = lax.broadcasted_iota(jnp.int32, (1, LANES), 1)
    a_rows = [jnp.where(lane < N_HEADS, -jnp.exp(alog_ref[d:d + 1, :]) * LOG2_E, 0.0)
              for d in range(2)]

    row_i = lax.broadcasted_iota(jnp.int32, (CHUNK, CHUNK), 0)
    col_i = lax.broadcasted_iota(jnp.int32, (CHUNK, CHUNK), 1)
    causal = [row_i >= col_i, row_i <= col_i]
    tri = [jnp.where(m, 1.0, 0.0).astype(BF16) for m in causal]
    glane = lax.broadcasted_iota(jnp.int32, (CHUNK, GROUP_W), 1)
    head_lanes = [(glane >= hh * HEAD_DIM) & (glane < (hh + 1) * HEAD_DIM)
                  for hh in range(HEADS_PER_GROUP)]
    expand = expand_ref[...]

    tables = []
    for d in range(2):
        for c in range(nc):
            rows = slice(c * CHUNK, (c + 1) * CHUNK)
            dt_c = dt_scr[d, rows, :]
            tables.append((d, c, rows, dt_c, _dot_exact_lhs(tri[d], dt_c * a_rows[d], 3)))
    for d, c, rows, dt_c, cum in tables:
        edge = CHUNK - 1 if d == 0 else 0
        edge_slab = slice(CHUNK - SUBLANES, CHUNK) if d == 0 else slice(0, SUBLANES)
        cum_scr[d, rows, :] = cum
        srct_scr[d, c] = (cum.T - jnp.log2(dt_c.T))[0:N_HEADS, :]
        wt_scr[d, c] = (dt_c * jnp.exp2(cum[edge:edge + 1, :] - cum)).T[0:N_HEADS, :]
        sdec_scr[d, c] = _dot_exact_rhs(jnp.exp2(cum[edge_slab, :]), expand, 2)

    def ssd_chunk(i, d, zero_state):
        c = i if d == 0 else nc - 1 - i
        rows = (slice(c * CHUNK, (c + 1) * CHUNK) if isinstance(c, int)
                else pl.ds(pl.multiple_of(c * CHUNK, CHUNK), CHUNK))
        edge = CHUNK - 1 if d == 0 else 0
        cum = cum_scr[d, rows, :]
        src_t = srct_scr[d, c]
        w_t = wt_scr[d, c]
        st_decay = sdec_scr[d, c, edge % SUBLANES:edge % SUBLANES + 1, :]
        xs_c = xs_scr[rows, :]
        xs_b = xs_c.astype(BF16)
        y_parts, st_parts = [], []
        for g in range(N_GROUPS):
            cs = slice(g * GROUP_W, (g + 1) * GROUP_W)
            ns = slice(g * D_STATE, (g + 1) * D_STATE)
            c_g = c_scr[rows, ns]
            bt_g = bt_scr[c, ns, :]
            cb = _dot(c_g.astype(BF16), bt_g.astype(BF16))
            if not zero_state:
                st_g = st_scr[d, :, cs]
                st_b = st_g.astype(BF16)
            x_g = xs_b[:, cs]
            zero = jnp.zeros_like(x_g)
            lhs_y, rhs_y, lhs_s, rhs_s = [], [], [], []
            for hh in range(HEADS_PER_GROUP):
                h = g * HEADS_PER_GROUP + hh
                cum_l = jnp.broadcast_to(cum[:, h:h + 1], (CHUNK, CHUNK))
                seg = jnp.where(causal[d], cum_l - src_t[h:h + 1, :], -jnp.inf)
                x_h = jnp.where(head_lanes[hh], x_g, zero)
                lhs_y.append((cb * jnp.exp2(seg)).astype(BF16))
                rhs_y.append(x_h)
                if not zero_state:
                    lhs_y.append((c_g * jnp.exp2(cum_l)).astype(BF16))
                    rhs_y.append(jnp.where(head_lanes[hh], st_b, zero))
                lhs_s.append((bt_g * w_t[h:h + 1, :]).astype(BF16))
                rhs_s.append(x_h)
            y_parts.append(_dot(jnp.concatenate(lhs_y, axis=1), jnp.concatenate(rhs_y, axis=0)))
            st_new = _dot(jnp.concatenate(lhs_s, axis=1), jnp.concatenate(rhs_s, axis=0))
            st_parts.append(st_new if zero_state else st_g * st_decay[:, cs] + st_new)
        return rows, y_parts, st_parts

    def ssd_step(i, carry, zero_state=False):
        results = [ssd_chunk(i, d, zero_state) for d in range(2)]
        for d, (rows, y_parts, st_parts) in enumerate(results):
            for g in range(N_GROUPS):
                st_scr[d, :, g * GROUP_W:(g + 1) * GROUP_W] = st_parts[g]
                for k in range(GROUP_W // LANES):
                    kt = g * (GROUP_W // LANES) + k
                    y_scr[kt, rows, :] = y_scr[kt, rows, :] + y_parts[g][:, k * LANES:(k + 1) * LANES]
        return carry

    if has_h0:
        for d in range(2):
            for k in range(D_SSM // D_STATE):
                ks = slice(k * D_STATE, (k + 1) * D_STATE)
                st_scr[d, :, ks] = h0_ref[0, 0, d, ks, :].T
        lax.fori_loop(0, nc, ssd_step, 0, unroll=2)
    else:
        ssd_step(0, 0, zero_state=True)
        lax.fori_loop(1, nc, ssd_step, 0)
    if want_state:
        for d in range(2):
            for k in range(D_SSM // D_STATE):
                ks = slice(k * D_STATE, (k + 1) * D_STATE)
                st_ref[0, 0, d, ks, :] = st_scr[d, :, ks].T

    n_lane_tiles = D_SSM // LANES
    if col_major:
        rows_per_col = L // GRID_W
        for r in range(rows_per_col):
            for kt in range(n_lane_tiles):
                xs_scr[r * GRID_W:(r + 1) * GRID_W, kt * LANES:(kt + 1) * LANES] = (
                    y_scr[kt, pl.ds(r, GRID_W, stride=rows_per_col), :])

    def tail(t, carry):
        rows = row_tile(t)
        hn = h_scr[rows, :]
        if col_major:
            y_t = xs_scr[rows, :]
        else:
            y_t = jnp.concatenate([y_scr[kt, rows, :] for kt in range(n_lane_tiles)], axis=1)
        yz = y_t * _silu(_dot(hn, wzm_ref[...]))
        ym = (yz * lax.rsqrt(_mean_last(yz * yz) + EPS) * ssmw_ref[...]).astype(BF16)
        merged = (_sigmoid(_dot(hn, wglm_ref[...])) * _dot(ym, woutm_ref[...])
                  + out_ref[0, rows, :])
        o = x_ref[0, rows, :] + gate * _dot(merged.astype(BF16), wout_ref[...])
        if final_norm:
            o = o * lax.rsqrt(_mean_last(o * o) + EPS) * finw_ref[...]
        out_ref[0, rows, :] = o
        return carry

    over_row_tiles(tail)


def _param_spec(arr, layer):
    nd = arr.ndim
    if layer is None:
        return pl.BlockSpec(arr.shape, lambda b: (0,) * nd, pipeline_mode=pl.Buffered(1))
    return pl.BlockSpec((1,) + arr.shape[1:], lambda b: (layer,) + (0,) * (nd - 1),
                        pipeline_mode=pl.Buffered(1))


def _layer_call(x, mod, h0, weights, *, layer, state_buf, col_major, want_state, final_norm,
                mod_per_seq):
    nb, L, _ = x.shape
    nc = L // CHUNK
    has_h0 = h0 is not None
    state_aliased = want_state and state_buf is not None
    kernel = functools.partial(_layer_kernel, seq_len=L, col_major=col_major, has_h0=has_h0,
                               want_state=want_state, final_norm=final_norm,
                               state_aliased=state_aliased)
    single = pl.Buffered(1) if L * D_MODEL * 4 >= (4 << 20) else None
    state_block = (1, 1, 2, D_SSM, D_STATE)
    state_index = lambda b: (b, layer, 0, 0, 0)
    in_specs = [pl.BlockSpec((1, L, D_MODEL), lambda b: (b, 0, 0), pipeline_mode=single),
                pl.BlockSpec((1, 1, 3, D_MODEL), (lambda b: (layer, 1 + b, 0, 0)) if mod_per_seq
                             else (lambda b: (layer, 0, 0, 0)))]
    args = [x, mod]
    if has_h0:
        in_specs.append(pl.BlockSpec(state_block, state_index))
        args.append(h0)
    stacked, shared = weights
    for w in stacked:
        if w.ndim == 2:
            in_specs.append(pl.BlockSpec((D_MODEL, w.shape[1]), lambda b: (layer, 0),
                                         pipeline_mode=pl.Buffered(1)))
        else:
            in_specs.append(_param_spec(w, layer))
        args.append(w)
    for w in shared:
        in_specs.append(_param_spec(w, None))
        args.append(w)
    aliases = {}
    if state_aliased:
        aliases[len(args)] = 1
        in_specs.append(pl.BlockSpec(memory_space=pl.ANY))
        args.append(state_buf)
    out_shape = [jax.ShapeDtypeStruct((nb, L, D_MODEL), F32)]
    out_specs = [pl.BlockSpec((1, L, D_MODEL), lambda b: (b, 0, 0), pipeline_mode=single)]
    if want_state:
        out_shape.append(jax.ShapeDtypeStruct((nb, DEPTH, 2, D_SSM, D_STATE), F32))
        out_specs.append(pl.BlockSpec(state_block, state_index))
    scratch = [
        pltpu.VMEM((L, D_MODEL), BF16),
        pltpu.VMEM((L if col_major else SUBLANES * 2, D_MODEL), BF16),
        pltpu.VMEM((2, L + 2 * CONV_PAD, CONV_TILE), F32),
        pltpu.VMEM((L, D_SSM), F32),
        pltpu.VMEM((L, N_GROUPS * D_STATE), F32),
        pltpu.VMEM((nc, N_GROUPS * D_STATE, CHUNK), F32),
        pltpu.VMEM((D_SSM // LANES, L, LANES), F32),
        pltpu.VMEM((2, D_STATE, D_SSM), F32),
        pltpu.VMEM((2, L, LANES), F32),
        pltpu.VMEM((ROW_TILE, D_GMLP), BF16),
        pltpu.VMEM((ROW_TILE, D_GMLP), F32),
        pltpu.VMEM((2, L, LANES), F32),
        pltpu.VMEM((2, nc, N_HEADS, CHUNK), F32),
        pltpu.VMEM((2, nc, N_HEADS, CHUNK), F32),
        pltpu.VMEM((2, nc, SUBLANES, D_SSM), F32),
    ]
    res = pl.pallas_call(
        kernel,
        grid=(nb,),
        in_specs=in_specs,
        out_specs=out_specs,
        out_shape=out_shape,
        scratch_shapes=scratch,
        input_output_aliases=aliases,
        compiler_params=pltpu.CompilerParams(
            dimension_semantics=("arbitrary",), vmem_limit_bytes=VMEM_LIMIT_BYTES),
        name="layer_L%d%s" % (L, "_cm" if col_major else ""),
    )(*args)
    return res if want_state else (res[0], None)


def _prep_weights(w_in, norm_w, conv_w, conv_b, dt_bias, a_log, d_skip, ssm_norm_w, w_out_m,
                  sgu_ln_w, sgu_ln_b, w_sp, b_sp, w_out_g, w_out, final_norm_w):
    w_a = w_in[:, :, :DT_LO].astype(BF16).reshape(DEPTH * D_MODEL, DT_LO)
    w_b = w_in[:, :, DT_HI:].astype(BF16).reshape(DEPTH * D_MODEL, -1)
    w_m, w_g, w_o = w_out_m.astype(BF16), w_out_g.astype(BF16), w_out.astype(BF16)
    w_dt = w_in[:, :, DT_LO:DT_HI].astype(BF16).reshape(DEPTH, D_MODEL, 2, N_HEADS)
    w_dt = jnp.pad(w_dt.transpose(0, 2, 1, 3), ((0, 0), (0, 0), (0, 0), (0, LANES - N_HEADS)))
    pad_h = ((0, 0), (0, 0), (0, LANES - N_HEADS))
    rows = lambda v: v.reshape(DEPTH, 1, -1)
    expand = (jnp.arange(LANES)[:, None] == (jnp.arange(D_SSM)[None, :] // HEAD_DIM)).astype(BF16)
    stacked = [
        w_a, w_b, w_dt, w_m, w_g, w_o, w_sp.astype(BF16),
        rows(norm_w), conv_w, rows(conv_b),
        jnp.pad(dt_bias, pad_h), jnp.pad(a_log, pad_h),
        rows(jnp.repeat(d_skip, HEAD_DIM, axis=-1)), rows(ssm_norm_w),
        rows(sgu_ln_w), rows(sgu_ln_b),
        jnp.repeat(b_sp.transpose(0, 2, 1), D_GMLP // N_GROUPS_GMLP, axis=-1),
    ]
    return stacked, [final_norm_w.reshape(1, -1), expand]


def kernel(x_prompt, x_sample, state_ssm, c, c_ctx, w_ada, b_ada, norm_w, w_in, conv_w, conv_b,
           dt_bias, a_log, d_skip, ssm_norm_w, w_out_m, sgu_ln_w, sgu_ln_b, w_sp, b_sp, w_out_g,
           w_out, final_norm_w):
    n_lat = c.shape[0]
    n_ctx = x_prompt.shape[0]
    cc = jnp.concatenate([c_ctx[None, :], c,
                          jnp.zeros((SUBLANES - 1 - n_lat, D_MODEL), F32)], axis=0)
    mod = _modulation(cc, w_ada, b_ada).reshape(DEPTH, SUBLANES, 3, D_MODEL)
    h0_lat = state_ssm.reshape(n_lat, DEPTH, 2, D_SSM, D_STATE)
    weights = _prep_weights(w_in, norm_w, conv_w, conv_b, dt_bias, a_log, d_skip, ssm_norm_w,
                            w_out_m, sgu_ln_w, sgu_ln_b, w_sp, b_sp, w_out_g, w_out, final_norm_w)

    h_ctx, h_lat = x_prompt, x_sample
    states = None
    for i in range(DEPTH):
        last = i == DEPTH - 1
        h_ctx, states = _layer_call(h_ctx, mod, None, weights, layer=i, state_buf=states,
                                    col_major=False, want_state=True, final_norm=last,
                                    mod_per_seq=False)
        h_lat, _ = _layer_call(h_lat, mod, h0_lat, weights, layer=i,
                               state_buf=None, col_major=(i % 2 == 1), want_state=False,
                               final_norm=last, mod_per_seq=True)
    return h_ctx, h_lat, states.reshape(n_ctx, DEPTH, 2, N_HEADS, HEAD_DIM, D_STATE)
```

```python
import functools
import itertools

import jax
import jax.numpy as jnp
import numpy as np
from jax import lax
from jax.experimental import pallas as pl
from jax.experimental.pallas import tpu as pltpu

F32 = jnp.float32
BF16 = jnp.bfloat16

D_MODEL = 1024
DEPTH = 4
GRID_W = 64
D_SSM = 1024
HEAD_DIM = 64
N_HEADS = 16
N_GROUPS = 4
HEADS_PER_GROUP = N_HEADS // N_GROUPS
D_STATE = 128
D_CONV = 5
CHUNK = 128
CONV_DIM = D_SSM + 2 * N_GROUPS * D_STATE
D_GMLP = 1024
N_GROUPS_GMLP = 8
DT_LO = CONV_DIM + D_SSM
DT_HI = DT_LO + 2 * N_HEADS
_A_NAMES, _A_SIZES = ("xbc", "zm"), (CONV_DIM, D_SSM)
_B_NAMES, _B_SIZES = ("u", "v", "zg", "glm", "glg"), (D_GMLP, D_GMLP, D_GMLP, D_MODEL, D_MODEL)
_spans = lambda names, sizes: {n: (int(hi - sz), int(hi))
                               for n, hi, sz in zip(names, np.cumsum(sizes), sizes)}
W_A_SEGMENTS = _spans(_A_NAMES, _A_SIZES)
W_B_SEGMENTS = _spans(_B_NAMES, _B_SIZES)
EPS = 1e-6

LANES = 128
SUBLANES = 8
CONV_PAD = SUBLANES
CONV_TILE = 256
GROUP_W = HEADS_PER_GROUP * HEAD_DIM
ROW_TILE = 256
VMEM_LIMIT_BYTES = 60 * 1024 * 1024
LOG2_E = np.float32(1.4426950408889634)


def _dot(a, b):
    return jnp.dot(a, b, preferred_element_type=F32)


def _split_bf16(a, parts):
    out = []
    rem = a
    for i in range(parts):
        p = rem.astype(BF16)
        out.append(p)
        if i + 1 < parts:
            rem = rem - p.astype(F32)
    return out


def _dot_exact_rhs(a, m_bf16, parts):
    acc = None
    for p in _split_bf16(a, parts):
        t = _dot(p, m_bf16)
        acc = t if acc is None else acc + t
    return acc


def _dot_exact_lhs(m_bf16, a, parts):
    acc = None
    for p in _split_bf16(a, parts):
        t = _dot(m_bf16, p)
        acc = t if acc is None else acc + t
    return acc


def _sigmoid(x):
    return 1.0 / (1.0 + jnp.exp(-x))


def _silu(x):
    return x * _sigmoid(x)


def _gelu_tanh(x):
    c = np.float32(np.sqrt(2.0 / np.pi))
    return x * (0.5 * (1.0 + jnp.tanh(c * (x + 0.044715 * (x * x * x)))))


def _softplus(x):
    return jnp.maximum(x, 0.0) + jnp.log1p(jnp.exp(-jnp.abs(x)))


def _mean_last(x):
    return jnp.mean(x, axis=-1, keepdims=True)


def _mod_kernel(c_ref, w_ref, b_ref, o_ref):
    sc = _silu(c_ref[...]).astype(BF16)
    o_ref[0] = _dot(sc, w_ref[0].astype(BF16)) + b_ref[0]


def _modulation(cc, w_ada, b_ada):
    rows = cc.shape[0]
    ncol = 3 * D_MODEL // D_MODEL
    return pl.pallas_call(
        _mod_kernel,
        grid=(DEPTH, ncol),
        in_specs=[
            pl.BlockSpec((rows, D_MODEL), lambda i, j: (0, 0)),
            pl.BlockSpec((1, D_MODEL, D_MODEL), lambda i, j: (i, 0, j)),
            pl.BlockSpec((1, 1, D_MODEL), lambda i, j: (i, 0, j)),
        ],
        out_specs=pl.BlockSpec((1, rows, D_MODEL), lambda i, j: (i, 0, j)),
        out_shape=jax.ShapeDtypeStruct((DEPTH, rows, 3 * D_MODEL), F32),
        name="adaln_mod",
    )(cc, w_ada, b_ada.reshape(DEPTH, 1, 3 * D_MODEL))


def _layer_kernel(*refs, seq_len, col_major, has_h0, want_state, final_norm, state_aliased):
    L = seq_len
    nc = L // CHUNK
    it = iter(refs)
    x_ref = next(it)
    mod_ref = next(it)
    h0_ref = next(it) if has_h0 else None
    (wa_ref, wb_ref, wdt_ref, woutm_ref, woutg_ref, wout_ref, wsp_ref,
     normw_ref, convw_ref, convb_ref, dtb_ref, alog_ref, dskip_ref, ssmw_ref,
     lnw_ref, lnb_ref, bsp_ref, finw_ref, expand_ref) = (next(it) for _ in range(19))
    w_seg = {name: wa_ref.at[0, :, lo:hi] for name, (lo, hi) in W_A_SEGMENTS.items()}
    w_seg.update({name: wb_ref.at[0, :, lo:hi] for name, (lo, hi) in W_B_SEGMENTS.items()})
    wxbc_ref, wzm_ref, wu_ref, wv_ref = w_seg["xbc"], w_seg["zm"], w_seg["u"], w_seg["v"]
    wzg_ref, wglm_ref, wglg_ref = w_seg["zg"], w_seg["glm"], w_seg["glg"]
    wdt_ref, woutm_ref, woutg_ref, wout_ref, wsp_ref = (
        r.at[0] for r in (wdt_ref, woutm_ref, woutg_ref, wout_ref, wsp_ref))
    (normw_ref, convw_ref, convb_ref, dtb_ref, alog_ref, dskip_ref, ssmw_ref, lnw_ref, lnb_ref,
     bsp_ref) = (r.at[0] for r in (normw_ref, convw_ref, convb_ref, dtb_ref, alog_ref, dskip_ref,
                                   ssmw_ref, lnw_ref, lnb_ref, bsp_ref))
    if state_aliased:
        next(it)
    out_ref = next(it)
    st_ref = next(it) if want_state else None
    (h_scr, hcm_scr, xraw_scr, xs_scr, c_scr, bt_scr, y_scr, st_scr,
     dt_scr, v_scr, s_scr, cum_scr, srct_scr, wt_scr, sdec_scr) = (next(it) for _ in range(15))

    shift = mod_ref[0, 0, 0:1, :]
    scale = mod_ref[0, 0, 1:2, :]
    gate = mod_ref[0, 0, 2:3, :]
    normw = normw_ref[...]
    n_row_tiles = L // ROW_TILE

    def norm_mod(xt):
        y = xt * lax.rsqrt(_mean_last(xt * xt) + EPS) * normw
        return (y * (1.0 + scale) + shift).astype(BF16)

    def row_tile(t):
        if isinstance(t, int):
            return slice(t * ROW_TILE, (t + 1) * ROW_TILE)
        return pl.ds(pl.multiple_of(t * ROW_TILE, ROW_TILE), ROW_TILE)

    def over_row_tiles(body):
        if n_row_tiles == 1:
            body(0, 0)
        else:
            lax.fori_loop(0, n_row_tiles, body, 0)

    for r0 in range(0, L, ROW_TILE):
        h_scr[r0:r0 + ROW_TILE, :] = norm_mod(x_ref[0, r0:r0 + ROW_TILE, :])
    if col_major:
        rows_per_col = L // GRID_W
        for r0 in range(0, L, ROW_TILE):
            dst = r0 + lax.broadcasted_iota(jnp.int32, (ROW_TILE, L), 0)
            src = lax.broadcasted_iota(jnp.int32, (ROW_TILE, L), 1)
            want = (dst % rows_per_col) * GRID_W + dst // rows_per_col
            perm = jnp.where(src == want, 1.0, 0.0).astype(BF16)
            hcm_scr[r0:r0 + ROW_TILE, :] = _dot(perm, h_scr[...]).astype(BF16)
        hsrc = hcm_scr
    else:
        hsrc = h_scr

    def gmlp_steps(t):
        rows = row_tile(t)
        hn = h_scr[rows, :]
        v = _gelu_tanh(_dot(hn, wv_ref[...]))
        vc = v - _mean_last(v)
        v_scr[...] = (vc * lax.rsqrt(_mean_last(vc * vc) + EPS) * lnw_ref[...]
                      + lnb_ref[...]).astype(BF16)
        yield
        for cc in range(ROW_TILE // CHUNK):
            for g in range(N_GROUPS_GMLP):
                blk = (slice(cc * CHUNK, (cc + 1) * CHUNK), slice(g * CHUNK, (g + 1) * CHUNK))
                s_scr[blk] = _dot(wsp_ref[g], v_scr[blk]) + bsp_ref[:, blk[1]]
            yield
        u = _gelu_tanh(_dot(hn, wu_ref[...]))
        yield
        yg = (u * s_scr[...] * _silu(_dot(hn, wzg_ref[...]))).astype(BF16)
        yield
        og = _dot(yg, woutg_ref[...])
        yield
        out_ref[0, rows, :] = _sigmoid(_dot(hn, wglg_ref[...])) * og
        yield

    def gmlp_tile(t, carry):
        for _ in gmlp_steps(t):
            pass
        return carry

    zero_pad = jnp.zeros((CONV_PAD, CONV_TILE), F32)
    for buf in range(2):
        xraw_scr[buf, 0:CONV_PAD, :] = zero_pad
        xraw_scr[buf, CONV_PAD + L:CONV_PAD + L + CONV_PAD, :] = zero_pad
    half = D_CONV // 2
    n_conv_tiles = CONV_DIM // CONV_TILE
    b_tile0 = D_SSM // CONV_TILE
    c_tile0 = (D_SSM + N_GROUPS * D_STATE) // CONV_TILE

    def project_conv_tile(j):
        cols = slice(j * CONV_TILE, (j + 1) * CONV_TILE)
        xraw_scr[j % 2, CONV_PAD:CONV_PAD + L, :] = _dot(hsrc[...], wxbc_ref[:, cols])

    def conv_tile(j):
        cols = slice(j * CONV_TILE, (j + 1) * CONV_TILE)
        for r0 in range(0, L, ROW_TILE):
            acc = convb_ref[:, cols]
            for k in range(D_CONV):
                start = CONV_PAD + r0 + k - half
                acc = acc + convw_ref[k:k + 1, cols] * xraw_scr[j % 2, start:start + ROW_TILE, :]
            act = _silu(acc)
            if j < b_tile0:
                xs_scr[r0:r0 + ROW_TILE, cols] = act
                skip = dskip_ref[:, cols] * act
                for k in range(CONV_TILE // LANES):
                    y_scr[j * (CONV_TILE // LANES) + k, r0:r0 + ROW_TILE, :] = (
                        skip[:, k * LANES:(k + 1) * LANES])
            elif j < c_tile0:
                bcols = (j - b_tile0) * CONV_TILE
                for cc in range(ROW_TILE // CHUNK):
                    for gg in range(CONV_TILE // D_STATE):
                        blk = act[cc * CHUNK:(cc + 1) * CHUNK, gg * D_STATE:(gg + 1) * D_STATE]
                        bt_scr[r0 // CHUNK + cc,
                               bcols + gg * D_STATE:bcols + (gg + 1) * D_STATE, :] = blk.T
            else:
                ccols = (j - c_tile0) * CONV_TILE
                c_scr[r0:r0 + ROW_TILE, ccols:ccols + CONV_TILE] = act

    def conv_steps():
        project_conv_tile(0)
        for j in range(n_conv_tiles):
            if j + 1 < n_conv_tiles:
                project_conv_tile(j + 1)
            conv_tile(j)
            yield

    def dt_steps():
        for d in range(2):
            dt_scr[d] = _softplus(_dot(hsrc[...], wdt_ref[d]) + dtb_ref[d:d + 1, :])
        yield

    if n_row_tiles == 1:
        for _ in itertools.zip_longest(itertools.chain(gmlp_steps(0), dt_steps()), conv_steps()):
            pass
    else:
        lax.fori_loop(0, n_row_tiles, gmlp_tile, 0)
        for _ in itertools.chain(conv_steps(), dt_steps()):
            pass

    lane = lax.broadcasted_iota(jnp.int32, (1, LANES), 1)
    a_rows = [jnp.where(lane < N_HEADS, -jnp.exp(alog_ref[d:d + 1, :]) * LOG2_E, 0.0)
              for d in range(2)]

    row_i = lax.broadcasted_iota(jnp.int32, (CHUNK, CHUNK), 0)
    col_i = lax.broadcasted_iota(jnp.int32, (CHUNK, CHUNK), 1)
    causal = [row_i >= col_i, row_i <= col_i]
    tri = [jnp.where(m, 1.0, 0.0).astype(BF16) for m in causal]
    glane = lax.broadcasted_iota(jnp.int32, (CHUNK, GROUP_W), 1)
    head_lanes = [(glane >= hh * HEAD_DIM) & (glane < (hh + 1) * HEAD_DIM)
                  for hh in range(HEADS_PER_GROUP)]
    expand = expand_ref[...]

    tables = []
    for d in range(2):
        for c in range(nc):
            rows = slice(c * CHUNK, (c + 1) * CHUNK)
            dt_c = dt_scr[d, rows, :]
            tables.append((d, c, rows, dt_c, _dot_exact_lhs(tri[d], dt_c * a_rows[d], 3)))
    for d, c, rows, dt_c, cum in tables:
        edge = CHUNK - 1 if d == 0 else 0
        edge_slab = slice(CHUNK - SUBLANES, CHUNK) if d == 0 else slice(0, SUBLANES)
        cum_scr[d, rows, :] = cum
        srct_scr[d, c] = (cum.T - jnp.log2(dt_c.T))[0:N_HEADS, :]
        wt_scr[d, c] = (dt_c * jnp.exp2(cum[edge:edge + 1, :] - cum)).T[0:N_HEADS, :]
        sdec_scr[d, c] = _dot_exact_rhs(jnp.exp2(cum[edge_slab, :]), expand, 2)

    cb_memo = {}

    def ssd_chunk(i, d, zero_state):
        c = i if d == 0 else nc - 1 - i
        rows = (slice(c * CHUNK, (c + 1) * CHUNK) if isinstance(c, int)
                else pl.ds(pl.multiple_of(c * CHUNK, CHUNK), CHUNK))
        edge = CHUNK - 1 if d == 0 else 0
        cum = cum_scr[d, rows, :]
        src_t = srct_scr[d, c]
        w_t = wt_scr[d, c]
        st_decay = sdec_scr[d, c, edge % SUBLANES:edge % SUBLANES + 1, :]
        xs_c = xs_scr[rows, :]
        xs_b = xs_c.astype(BF16)
        y_parts, st_parts = [], []
        for g in range(N_GROUPS):
            cs = slice(g * GROUP_W, (g + 1) * GROUP_W)
            ns = slice(g * D_STATE, (g + 1) * D_STATE)
            c_g = c_scr[rows, ns]
            bt_g = bt_scr[c, ns, :]
            if isinstance(c, int) and (c, g) in cb_memo:
                cb = cb_memo[(c, g)]
            else:
                cb = _dot(c_g.astype(BF16), bt_g.astype(BF16))
                if isinstance(c, int):
                    cb_memo[(c, g)] = cb
            if not zero_state:
                st_g = st_scr[d, :, cs]
                st_b = st_g.astype(BF16)
            x_g = xs_b[:, cs]
            zero = jnp.zeros_like(x_g)
            lhs_y, rhs_y, lhs_s, rhs_s = [], [], [], []
            for hh in range(HEADS_PER_GROUP):
                h = g * HEADS_PER_GROUP + hh
                cum_l = jnp.broadcast_to(cum[:, h:h + 1], (CHUNK, CHUNK))
                seg = jnp.where(causal[d], cum_l - src_t[h:h + 1, :], -jnp.inf)
                x_h = jnp.where(head_lanes[hh], x_g, zero)
                lhs_y.append((cb * jnp.exp2(seg)).astype(BF16))
                rhs_y.append(x_h)
                if not zero_state:
                    lhs_y.append((c_g * jnp.exp2(cum_l)).astype(BF16))
                    rhs_y.append(jnp.where(head_lanes[hh], st_b, zero))
                lhs_s.append((bt_g * w_t[h:h + 1, :]).astype(BF16))
                rhs_s.append(x_h)
            y_parts.append(_dot(jnp.concatenate(lhs_y, axis=1), jnp.concatenate(rhs_y, axis=0)))
            st_new = _dot(jnp.concatenate(lhs_s, axis=1), jnp.concatenate(rhs_s, axis=0))
            st_parts.append(st_new if zero_state else st_g * st_decay[:, cs] + st_new)
        return rows, y_parts, st_parts

    def ssd_step(i, carry, zero_state=False):
        results = [ssd_chunk(i, d, zero_state) for d in range(2)]
        for d, (rows, y_parts, st_parts) in enumerate(results):
            for g in range(N_GROUPS):
                st_scr[d, :, g * GROUP_W:(g + 1) * GROUP_W] = st_parts[g]
                for k in range(GROUP_W // LANES):
                    kt = g * (GROUP_W // LANES) + k
                    y_scr[kt, rows, :] = y_scr[kt, rows, :] + y_parts[g][:, k * LANES:(k + 1) * LANES]
        return carry

    if has_h0:
        for d in range(2):
            for k in range(D_SSM // D_STATE):
                ks = slice(k * D_STATE, (k + 1) * D_STATE)
                st_scr[d, :, ks] = h0_ref[0, 0, d, ks, :].T
        lax.fori_loop(0, nc, ssd_step, 0, unroll=2)
    else:
        ssd_step(0, 0, zero_state=True)
        for i in range(1, nc):
            ssd_step(i, 0)
    if want_state:
        for d in range(2):
            for k in range(D_SSM // D_STATE):
                ks = slice(k * D_STATE, (k + 1) * D_STATE)
                st_ref[0, 0, d, ks, :] = st_scr[d, :, ks].T

    n_lane_tiles = D_SSM // LANES
    if col_major:
        rows_per_col = L // GRID_W
        for r in range(rows_per_col):
            for kt in range(n_lane_tiles):
                xs_scr[r * GRID_W:(r + 1) * GRID_W, kt * LANES:(kt + 1) * LANES] = (
                    y_scr[kt, pl.ds(r, GRID_W, stride=rows_per_col), :])

    def tail(t, carry):
        rows = row_tile(t)
        hn = h_scr[rows, :]
        if col_major:
            y_t = xs_scr[rows, :]
        else:
            y_t = jnp.concatenate([y_scr[kt, rows, :] for kt in range(n_lane_tiles)], axis=1)
        yz = y_t * _silu(_dot(hn, wzm_ref[...]))
        ym = (yz * lax.rsqrt(_mean_last(yz * yz) + EPS) * ssmw_ref[...]).astype(BF16)
        merged = (_sigmoid(_dot(hn, wglm_ref[...])) * _dot(ym, woutm_ref[...])
                  + out_ref[0, rows, :])
        o = x_ref[0, rows, :] + gate * _dot(merged.astype(BF16), wout_ref[...])
        if final_norm:
            o = o * lax.rsqrt(_mean_last(o * o) + EPS) * finw_ref[...]
        out_ref[0, rows, :] = o
        return carry

    over_row_tiles(tail)


def _param_spec(arr, layer):
    nd = arr.ndim
    if layer is None:
        return pl.BlockSpec(arr.shape, lambda b: (0,) * nd, pipeline_mode=pl.Buffered(1))
    return pl.BlockSpec((1,) + arr.shape[1:], lambda b: (layer,) + (0,) * (nd - 1),
                        pipeline_mode=pl.Buffered(1))


def _layer_call(x, mod, h0, weights, *, layer, state_buf, col_major, want_state, final_norm,
                mod_per_seq):
    nb, L, _ = x.shape
    nc = L // CHUNK
    has_h0 = h0 is not None
    state_aliased = want_state and state_buf is not None
    kernel = functools.partial(_layer_kernel, seq_len=L, col_major=col_major, has_h0=has_h0,
                               want_state=want_state, final_norm=final_norm,
                               state_aliased=state_aliased)
    single = pl.Buffered(1) if L * D_MODEL * 4 >= (4 << 20) else None
    state_block = (1, 1, 2, D_SSM, D_STATE)
    state_index = lambda b: (b, layer, 0, 0, 0)
    in_specs = [pl.BlockSpec((1, L, D_MODEL), lambda b: (b, 0, 0), pipeline_mode=single),
                pl.BlockSpec((1, 1, 3, D_MODEL), (lambda b: (layer, 1 + b, 0, 0)) if mod_per_seq
                             else (lambda b: (layer, 0, 0, 0)))]
    args = [x, mod]
    if has_h0:
        in_specs.append(pl.BlockSpec(state_block, state_index))
        args.append(h0)
    stacked, shared = weights
    for w in stacked:
        in_specs.append(_param_spec(w, layer))
        args.append(w)
    for w in shared:
        in_specs.append(_param_spec(w, None))
        args.append(w)
    aliases = {}
    if state_aliased:
        aliases[len(args)] = 1
        in_specs.append(pl.BlockSpec(memory_space=pl.ANY))
        args.append(state_buf)
    out_shape = [jax.ShapeDtypeStruct((nb, L, D_MODEL), F32)]
    out_specs = [pl.BlockSpec((1, L, D_MODEL), lambda b: (b, 0, 0), pipeline_mode=single)]
    if want_state:
        out_shape.append(jax.ShapeDtypeStruct((nb, DEPTH, 2, D_SSM, D_STATE), F32))
        out_specs.append(pl.BlockSpec(state_block, state_index))
    scratch = [
        pltpu.VMEM((L, D_MODEL), BF16),
        pltpu.VMEM((L if col_major else SUBLANES * 2, D_MODEL), BF16),
        pltpu.VMEM((2, L + 2 * CONV_PAD, CONV_TILE), F32),
        pltpu.VMEM((L, D_SSM), F32),
        pltpu.VMEM((L, N_GROUPS * D_STATE), F32),
        pltpu.VMEM((nc, N_GROUPS * D_STATE, CHUNK), F32),
        pltpu.VMEM((D_SSM // LANES, L, LANES), F32),
        pltpu.VMEM((2, D_STATE, D_SSM), F32),
        pltpu.VMEM((2, L, LANES), F32),
        pltpu.VMEM((ROW_TILE, D_GMLP), BF16),
        pltpu.VMEM((ROW_TILE, D_GMLP), F32),
        pltpu.VMEM((2, L, LANES), F32),
        pltpu.VMEM((2, nc, N_HEADS, CHUNK), F32),
        pltpu.VMEM((2, nc, N_HEADS, CHUNK), F32),
        pltpu.VMEM((2, nc, SUBLANES, D_SSM), F32),
    ]
    res = pl.pallas_call(
        kernel,
        grid=(nb,),
        in_specs=in_specs,
        out_specs=out_specs,
        out_shape=out_shape,
        scratch_shapes=scratch,
        input_output_aliases=aliases,
        compiler_params=pltpu.CompilerParams(
            dimension_semantics=("arbitrary",), vmem_limit_bytes=VMEM_LIMIT_BYTES),
        name="layer_L%d%s" % (L, "_cm" if col_major else ""),
    )(*args)
    return res if want_state else (res[0], None)


def _prep_weights(w_in, norm_w, conv_w, conv_b, dt_bias, a_log, d_skip, ssm_norm_w, w_out_m,
                  sgu_ln_w, sgu_ln_b, w_sp, b_sp, w_out_g, w_out, final_norm_w):
    w_a = w_in[:, :, :DT_LO].astype(BF16)
    w_b = w_in[:, :, DT_HI:].astype(BF16)
    w_m, w_g, w_o = w_out_m.astype(BF16), w_out_g.astype(BF16), w_out.astype(BF16)
    w_dt = w_in[:, :, DT_LO:DT_HI].astype(BF16).reshape(DEPTH, D_MODEL, 2, N_HEADS)
    w_dt = jnp.pad(w_dt.transpose(0, 2, 1, 3), ((0, 0), (0, 0), (0, 0), (0, LANES - N_HEADS)))
    pad_h = ((0, 0), (0, 0), (0, LANES - N_HEADS))
    rows = lambda v: v.reshape(DEPTH, 1, -1)
    expand = (jnp.arange(LANES)[:, None] == (jnp.arange(D_SSM)[None, :] // HEAD_DIM)).astype(BF16)
    stacked = [
        w_a, w_b, w_dt, w_m, w_g, w_o, w_sp.astype(BF16),
        rows(norm_w), conv_w, rows(conv_b),
        jnp.pad(dt_bias, pad_h), jnp.pad(a_log, pad_h),
        rows(jnp.repeat(d_skip, HEAD_DIM, axis=-1)), rows(ssm_norm_w),
        rows(sgu_ln_w), rows(sgu_ln_b),
        jnp.repeat(b_sp.transpose(0, 2, 1), D_GMLP // N_GROUPS_GMLP, axis=-1),
    ]
    return stacked, [final_norm_w.reshape(1, -1), expand]


def kernel(x_prompt, x_sample, state_ssm, c, c_ctx, w_ada, b_ada, norm_w, w_in, conv_w, conv_b,
           dt_bias, a_log, d_skip, ssm_norm_w, w_out_m, sgu_ln_w, sgu_ln_b, w_sp, b_sp, w_out_g,
           w_out, final_norm_w):
    n_lat = c.shape[0]
    n_ctx = x_prompt.shape[0]
    cc = jnp.concatenate([c_ctx[None, :], c,
                          jnp.zeros((SUBLANES - 1 - n_lat, D_MODEL), F32)], axis=0)
    mod = _modulation(cc, w_ada, b_ada).reshape(DEPTH, SUBLANES, 3, D_MODEL)
    h0_lat = state_ssm.reshape(n_lat, DEPTH, 2, D_SSM, D_STATE)
    weights = _prep_weights(w_in, norm_w, conv_w, conv_b, dt_bias, a_log, d_skip, ssm_norm_w,
                            w_out_m, sgu_ln_w, sgu_ln_b, w_sp, b_sp, w_out_g, w_out, final_norm_w)

    h_ctx, h_lat = x_prompt, x_sample
    states = None
    for i in range(DEPTH):
        last = i == DEPTH - 1
        h_ctx, states = _layer_call(h_ctx, mod, None, weights, layer=i, state_buf=states,
                                    col_major=False, want_state=True, final_norm=last,
                                    mod_per_seq=False)
        h_lat, _ = _layer_call(h_lat, mod, h0_lat, weights, layer=i,
                               state_buf=None, col_major=(i % 2 == 1), want_state=False,
                               final_norm=last, mod_per_seq=True)
    return h_ctx, h_lat, states.reshape(n_ctx, DEPTH, 2, N_HEADS, HEAD_DIM, D_STATE)
```

```python
import functools
import itertools

import jax
import jax.numpy as jnp
import numpy as np
from jax import lax
from jax.experimental import pallas as pl
from jax.experimental.pallas import tpu as pltpu

F32 = jnp.float32
BF16 = jnp.bfloat16

D_MODEL = 1024
DEPTH = 4
GRID_W = 64
D_SSM = 1024
HEAD_DIM = 64
N_HEADS = 16
N_GROUPS = 4
HEADS_PER_GROUP = N_HEADS // N_GROUPS
D_STATE = 128
D_CONV = 5
CHUNK = 128
CONV_DIM = D_SSM + 2 * N_GROUPS * D_STATE
D_GMLP = 1024
N_GROUPS_GMLP = 8
DT_LO = CONV_DIM + D_SSM
DT_HI = DT_LO + 2 * N_HEADS
_A_NAMES, _A_SIZES = ("xbc", "zm"), (CONV_DIM, D_SSM)
_B_NAMES, _B_SIZES = ("u", "v", "zg", "glm", "glg"), (D_GMLP, D_GMLP, D_GMLP, D_MODEL, D_MODEL)
_spans = lambda names, sizes: {n: (int(hi - sz), int(hi))
                               for n, hi, sz in zip(names, np.cumsum(sizes), sizes)}
W_A_SEGMENTS = _spans(_A_NAMES, _A_SIZES)
W_B_SEGMENTS = _spans(_B_NAMES, _B_SIZES)
EPS = 1e-6

LANES = 128
SUBLANES = 8
CONV_PAD = SUBLANES
CONV_TILE = 256
GROUP_W = HEADS_PER_GROUP * HEAD_DIM
ROW_TILE = 256
SEQS_PER_STEP = 2
VMEM_LIMIT_BYTES = 60 * 1024 * 1024
LOG2_E = np.float32(1.4426950408889634)


def _dot(a, b):
    return jnp.dot(a, b, preferred_element_type=F32)


def _split_bf16(a, parts):
    out = []
    rem = a
    for i in range(parts):
        p = rem.astype(BF16)
        out.append(p)
        if i + 1 < parts:
            rem = rem - p.astype(F32)
    return out


def _dot_exact_rhs(a, m_bf16, parts):
    acc = None
    for p in _split_bf16(a, parts):
        t = _dot(p, m_bf16)
        acc = t if acc is None else acc + t
    return acc


def _dot_exact_lhs(m_bf16, a, parts):
    acc = None
    for p in _split_bf16(a, parts):
        t = _dot(m_bf16, p)
        acc = t if acc is None else acc + t
    return acc


def _sigmoid(x):
    return 1.0 / (1.0 + jnp.exp(-x))


def _silu(x):
    return x * _sigmoid(x)


def _gelu_tanh(x):
    c = np.float32(np.sqrt(2.0 / np.pi))
    return x * (0.5 * (1.0 + jnp.tanh(c * (x + 0.044715 * (x * x * x)))))


def _softplus(x):
    return jnp.maximum(x, 0.0) + jnp.log1p(jnp.exp(-jnp.abs(x)))


def _mean_last(x):
    return jnp.mean(x, axis=-1, keepdims=True)


def _mod_kernel(c_ref, w_ref, b_ref, o_ref):
    sc = _silu(c_ref[...]).astype(BF16)
    o_ref[0] = _dot(sc, w_ref[0].astype(BF16)) + b_ref[0]


def _modulation(cc, w_ada, b_ada):
    rows = cc.shape[0]
    ncol = 3 * D_MODEL // D_MODEL
    return pl.pallas_call(
        _mod_kernel,
        grid=(DEPTH, ncol),
        in_specs=[
            pl.BlockSpec((rows, D_MODEL), lambda i, j: (0, 0)),
            pl.BlockSpec((1, D_MODEL, D_MODEL), lambda i, j: (i, 0, j)),
            pl.BlockSpec((1, 1, D_MODEL), lambda i, j: (i, 0, j)),
        ],
        out_specs=pl.BlockSpec((1, rows, D_MODEL), lambda i, j: (i, 0, j)),
        out_shape=jax.ShapeDtypeStruct((DEPTH, rows, 3 * D_MODEL), F32),
        name="adaln_mod",
    )(cc, w_ada, b_ada.reshape(DEPTH, 1, 3 * D_MODEL))


def _layer_kernel(*refs, seqs_per_step, n_inputs, n_outputs, **static):
    per_seq = {0} | set(range(n_inputs, n_inputs + n_outputs))
    for s in range(seqs_per_step):
        _layer_body(*[r.at[s:s + 1] if k in per_seq else r for k, r in enumerate(refs)], **static)


def _layer_body(*refs, seq_len, col_major, has_h0, want_state, final_norm, state_aliased):
    L = seq_len
    nc = L // CHUNK
    it = iter(refs)
    x_ref = next(it)
    mod_ref = next(it)
    h0_ref = next(it) if has_h0 else None
    (wa_ref, wb_ref, wdt_ref, woutm_ref, woutg_ref, wout_ref, wsp_ref,
     normw_ref, convw_ref, convb_ref, dtb_ref, alog_ref, dskip_ref, ssmw_ref,
     lnw_ref, lnb_ref, bsp_ref, finw_ref, expand_ref) = (next(it) for _ in range(19))
    w_seg = {name: wa_ref.at[0, :, lo:hi] for name, (lo, hi) in W_A_SEGMENTS.items()}
    w_seg.update({name: wb_ref.at[0, :, lo:hi] for name, (lo, hi) in W_B_SEGMENTS.items()})
    wxbc_ref, wzm_ref, wu_ref, wv_ref = w_seg["xbc"], w_seg["zm"], w_seg["u"], w_seg["v"]
    wzg_ref, wglm_ref, wglg_ref = w_seg["zg"], w_seg["glm"], w_seg["glg"]
    wdt_ref, woutm_ref, woutg_ref, wout_ref, wsp_ref = (
        r.at[0] for r in (wdt_ref, woutm_ref, woutg_ref, wout_ref, wsp_ref))
    (normw_ref, convw_ref, convb_ref, dtb_ref, alog_ref, dskip_ref, ssmw_ref, lnw_ref, lnb_ref,
     bsp_ref) = (r.at[0] for r in (normw_ref, convw_ref, convb_ref, dtb_ref, alog_ref, dskip_ref,
                                   ssmw_ref, lnw_ref, lnb_ref, bsp_ref))
    if state_aliased:
        next(it)
    out_ref = next(it)
    st_ref = next(it) if want_state else None
    (h_scr, hcm_scr, xraw_scr, xs_scr, c_scr, bt_scr, y_scr, st_scr,
     dt_scr, v_scr, s_scr, cum_scr, srct_scr, wt_scr, sdec_scr) = (next(it) for _ in range(15))

    shift = mod_ref[0, 0, 0:1, :]
    scale = mod_ref[0, 0, 1:2, :]
    gate = mod_ref[0, 0, 2:3, :]
    normw = normw_ref[...]
    n_row_tiles = L // ROW_TILE

    def norm_mod(xt):
        y = xt * lax.rsqrt(_mean_last(xt * xt) + EPS) * normw
        return (y * (1.0 + scale) + shift).astype(BF16)

    def row_tile(t):
        if isinstance(t, int):
            return slice(t * ROW_TILE, (t + 1) * ROW_TILE)
        return pl.ds(pl.multiple_of(t * ROW_TILE, ROW_TILE), ROW_TILE)

    def over_row_tiles(body):
        if n_row_tiles == 1:
            body(0, 0)
        else:
            lax.fori_loop(0, n_row_tiles, body, 0)

    for r0 in range(0, L, ROW_TILE):
        h_scr[r0:r0 + ROW_TILE, :] = norm_mod(x_ref[0, r0:r0 + ROW_TILE, :])
    if col_major:
        rows_per_col = L // GRID_W
        for r0 in range(0, L, ROW_TILE):
            dst = r0 + lax.broadcasted_iota(jnp.int32, (ROW_TILE, L), 0)
            src = lax.broadcasted_iota(jnp.int32, (ROW_TILE, L), 1)
            want = (dst % rows_per_col) * GRID_W + dst // rows_per_col
            perm = jnp.where(src == want, 1.0, 0.0).astype(BF16)
            hcm_scr[r0:r0 + ROW_TILE, :] = _dot(perm, h_scr[...]).astype(BF16)
        hsrc = hcm_scr
    else:
        hsrc = h_scr

    def gmlp_steps(t):
        rows = row_tile(t)
        hn = h_scr[rows, :]
        v = _gelu_tanh(_dot(hn, wv_ref[...]))
        vc = v - _mean_last(v)
        v_scr[...] = (vc * lax.rsqrt(_mean_last(vc * vc) + EPS) * lnw_ref[...]
                      + lnb_ref[...]).astype(BF16)
        yield
        for cc in range(ROW_TILE // CHUNK):
            for g in range(N_GROUPS_GMLP):
                blk = (slice(cc * CHUNK, (cc + 1) * CHUNK), slice(g * CHUNK, (g + 1) * CHUNK))
                s_scr[blk] = _dot(wsp_ref[g], v_scr[blk]) + bsp_ref[:, blk[1]]
            yield
        u = _gelu_tanh(_dot(hn, wu_ref[...]))
        yield
        yg = (u * s_scr[...] * _silu(_dot(hn, wzg_ref[...]))).astype(BF16)
        yield
        og = _dot(yg, woutg_ref[...])
        yield
        out_ref[0, rows, :] = _sigmoid(_dot(hn, wglg_ref[...])) * og
        yield

    def gmlp_tile(t, carry):
        for _ in gmlp_steps(t):
            pass
        return carry

    zero_pad = jnp.zeros((CONV_PAD, CONV_TILE), F32)
    for buf in range(2):
        xraw_scr[buf, 0:CONV_PAD, :] = zero_pad
        xraw_scr[buf, CONV_PAD + L:CONV_PAD + L + CONV_PAD, :] = zero_pad
    half = D_CONV // 2
    n_conv_tiles = CONV_DIM // CONV_TILE
    b_tile0 = D_SSM // CONV_TILE
    c_tile0 = (D_SSM + N_GROUPS * D_STATE) // CONV_TILE

    def project_conv_tile(j):
        cols = slice(j * CONV_TILE, (j + 1) * CONV_TILE)
        xraw_scr[j % 2, CONV_PAD:CONV_PAD + L, :] = _dot(hsrc[...], wxbc_ref[:, cols])

    def conv_tile(j):
        cols = slice(j * CONV_TILE, (j + 1) * CONV_TILE)
        for r0 in range(0, L, ROW_TILE):
            acc = convb_ref[:, cols]
            for k in range(D_CONV):
                start = CONV_PAD + r0 + k - half
                acc = acc + convw_ref[k:k + 1, cols] * xraw_scr[j % 2, start:start + ROW_TILE, :]
            act = _silu(acc)
            if j < b_tile0:
                xs_scr[r0:r0 + ROW_TILE, cols] = act
                skip = dskip_ref[:, cols] * act
                for k in range(CONV_TILE // LANES):
                    y_scr[j * (CONV_TILE // LANES) + k, r0:r0 + ROW_TILE, :] = (
                        skip[:, k * LANES:(k + 1) * LANES])
            elif j < c_tile0:
                bcols = (j - b_tile0) * CONV_TILE
                for cc in range(ROW_TILE // CHUNK):
                    for gg in range(CONV_TILE // D_STATE):
                        blk = act[cc * CHUNK:(cc + 1) * CHUNK, gg * D_STATE:(gg + 1) * D_STATE]
                        bt_scr[r0 // CHUNK + cc,
                               bcols + gg * D_STATE:bcols + (gg + 1) * D_STATE, :] = blk.T
            else:
                ccols = (j - c_tile0) * CONV_TILE
                c_scr[r0:r0 + ROW_TILE, ccols:ccols + CONV_TILE] = act

    def conv_steps():
        project_conv_tile(0)
        for j in range(n_conv_tiles):
            if j + 1 < n_conv_tiles:
                project_conv_tile(j + 1)
            conv_tile(j)
            yield

    def dt_steps():
        for d in range(2):
            dt_scr[d] = _softplus(_dot(hsrc[...], wdt_ref[d]) + dtb_ref[d:d + 1, :])
        yield

    if n_row_tiles == 1:
        for _ in itertools.zip_longest(itertools.chain(gmlp_steps(0), dt_steps()), conv_steps()):
            pass
    else:
        lax.fori_loop(0, n_row_tiles, gmlp_tile, 0)
        for _ in itertools.chain(conv_steps(), dt_steps()):
            pass

    lane = lax.broadcasted_iota(jnp.int32, (1, LANES), 1)
    a_rows = [jnp.where(lane < N_HEADS, -jnp.exp(alog_ref[d:d + 1, :]) * LOG2_E, 0.0)
              for d in range(2)]

    row_i = lax.broadcasted_iota(jnp.int32, (CHUNK, CHUNK), 0)
    col_i = lax.broadcasted_iota(jnp.int32, (CHUNK, CHUNK), 1)
    causal = [row_i >= col_i, row_i <= col_i]
    tri = [jnp.where(m, 1.0, 0.0).astype(BF16) for m in causal]
    glane = lax.broadcasted_iota(jnp.int32, (CHUNK, GROUP_W), 1)
    head_lanes = [(glane >= hh * HEAD_DIM) & (glane < (hh + 1) * HEAD_DIM)
                  for hh in range(HEADS_PER_GROUP)]
    expand = expand_ref[...]

    tables = []
    for d in range(2):
        for c in range(nc):
            rows = slice(c * CHUNK, (c + 1) * CHUNK)
            dt_c = dt_scr[d, rows, :]
            tables.append((d, c, rows, dt_c, _dot_exact_lhs(tri[d], dt_c * a_rows[d], 3)))
    for d, c, rows, dt_c, cum in tables:
        edge = CHUNK - 1 if d == 0 else 0
        edge_slab = slice(CHUNK - SUBLANES, CHUNK) if d == 0 else slice(0, SUBLANES)
        cum_scr[d, rows, :] = cum
        srct_scr[d, c] = (cum.T - jnp.log2(dt_c.T))[0:N_HEADS, :]
        wt_scr[d, c] = (dt_c * jnp.exp2(cum[edge:edge + 1, :] - cum)).T[0:N_HEADS, :]
        sdec_scr[d, c] = _dot_exact_rhs(jnp.exp2(cum[edge_slab, :]), expand, 2)

    cb_memo = {}

    def ssd_chunk(i, d, zero_state):
        c = i if d == 0 else nc - 1 - i
        rows = (slice(c * CHUNK, (c + 1) * CHUNK) if isinstance(c, int)
                else pl.ds(pl.multiple_of(c * CHUNK, CHUNK), CHUNK))
        edge = CHUNK - 1 if d == 0 else 0
        cum = cum_scr[d, rows, :]
        src_t = srct_scr[d, c]
        w_t = wt_scr[d, c]
        st_decay = sdec_scr[d, c, edge % SUBLANES:edge % SUBLANES + 1, :]
        xs_c = xs_scr[rows, :]
        xs_b = xs_c.astype(BF16)
        y_parts, st_parts = [], []
        for g in range(N_GROUPS):
            cs = slice(g * GROUP_W, (g + 1) * GROUP_W)
            ns = slice(g * D_STATE, (g + 1) * D_STATE)
            c_g = c_scr[rows, ns]
            bt_g = bt_scr[c, ns, :]
            if isinstance(c, int) and (c, g) in cb_memo:
                cb = cb_memo[(c, g)]
            else:
                cb = _dot(c_g.astype(BF16), bt_g.astype(BF16))
                if isinstance(c, int):
                    cb_memo[(c, g)] = cb
            if not zero_state:
                st_g = st_scr[d, :, cs]
                st_b = st_g.astype(BF16)
            x_g = xs_b[:, cs]
            zero = jnp.zeros_like(x_g)
            lhs_y, rhs_y, lhs_s, rhs_s = [], [], [], []
            for hh in range(HEADS_PER_GROUP):
                h = g * HEADS_PER_GROUP + hh
                cum_l = jnp.broadcast_to(cum[:, h:h + 1], (CHUNK, CHUNK))
                seg = jnp.where(causal[d], cum_l - src_t[h:h + 1, :], -jnp.inf)
                x_h = jnp.where(head_lanes[hh], x_g, zero)
                lhs_y.append((cb * jnp.exp2(seg)).astype(BF16))
                rhs_y.append(x_h)
                if not zero_state:
                    lhs_y.append((c_g * jnp.exp2(cum_l)).astype(BF16))
                    rhs_y.append(jnp.where(head_lanes[hh], st_b, zero))
                lhs_s.append((bt_g * w_t[h:h + 1, :]).astype(BF16))
                rhs_s.append(x_h)
            y_parts.append(_dot(jnp.concatenate(lhs_y, axis=1), jnp.concatenate(rhs_y, axis=0)))
            st_new = _dot(jnp.concatenate(lhs_s, axis=1), jnp.concatenate(rhs_s, axis=0))
            st_parts.append(st_new if zero_state else st_g * st_decay[:, cs] + st_new)
        return rows, y_parts, st_parts

    def ssd_step(i, carry, zero_state=False):
        results = [ssd_chunk(i, d, zero_state) for d in range(2)]
        for d, (rows, y_parts, st_parts) in enumerate(results):
            for g in range(N_GROUPS):
                st_scr[d, :, g * GROUP_W:(g + 1) * GROUP_W] = st_parts[g]
                for k in range(GROUP_W // LANES):
                    kt = g * (GROUP_W // LANES) + k
                    y_scr[kt, rows, :] = y_scr[kt, rows, :] + y_parts[g][:, k * LANES:(k + 1) * LANES]
        return carry

    if has_h0:
        for d in range(2):
            for k in range(D_SSM // D_STATE):
                ks = slice(k * D_STATE, (k + 1) * D_STATE)
                st_scr[d, :, ks] = h0_ref[0, 0, d, ks, :].T
        lax.fori_loop(0, nc, ssd_step, 0, unroll=2)
    else:
        ssd_step(0, 0, zero_state=True)
        for i in range(1, nc):
            ssd_step(i, 0)
    if want_state:
        for d in range(2):
            for k in range(D_SSM // D_STATE):
                ks = slice(k * D_STATE, (k + 1) * D_STATE)
                st_ref[0, 0, d, ks, :] = st_scr[d, :, ks].T

    n_lane_tiles = D_SSM // LANES
    if col_major:
        rows_per_col = L // GRID_W
        for r in range(rows_per_col):
            for kt in range(n_lane_tiles):
                xs_scr[r * GRID_W:(r + 1) * GRID_W, kt * LANES:(kt + 1) * LANES] = (
                    y_scr[kt, pl.ds(r, GRID_W, stride=rows_per_col), :])

    def tail(t, carry):
        rows = row_tile(t)
        hn = h_scr[rows, :]
        if col_major:
            y_t = xs_scr[rows, :]
        else:
            y_t = jnp.concatenate([y_scr[kt, rows, :] for kt in range(n_lane_tiles)], axis=1)
        yz = y_t * _silu(_dot(hn, wzm_ref[...]))
        ym = (yz * lax.rsqrt(_mean_last(yz * yz) + EPS) * ssmw_ref[...]).astype(BF16)
        merged = (_sigmoid(_dot(hn, wglm_ref[...])) * _dot(ym, woutm_ref[...])
                  + out_ref[0, rows, :])
        o = x_ref[0, rows, :] + gate * _dot(merged.astype(BF16), wout_ref[...])
        if final_norm:
            o = o * lax.rsqrt(_mean_last(o * o) + EPS) * finw_ref[...]
        out_ref[0, rows, :] = o
        return carry

    over_row_tiles(tail)


def _param_spec(arr, layer):
    nd = arr.ndim
    if layer is None:
        return pl.BlockSpec(arr.shape, lambda b: (0,) * nd, pipeline_mode=pl.Buffered(1))
    return pl.BlockSpec((1,) + arr.shape[1:], lambda b: (layer,) + (0,) * (nd - 1),
                        pipeline_mode=pl.Buffered(1))


def _layer_call(x, mod, h0, weights, *, layer, state_buf, col_major, want_state, final_norm,
                mod_per_seq):
    nb, L, _ = x.shape
    nc = L // CHUNK
    has_h0 = h0 is not None
    state_aliased = want_state and state_buf is not None
    sps = SEQS_PER_STEP if (not has_h0 and L <= ROW_TILE and nb % SEQS_PER_STEP == 0) else 1
    single = pl.Buffered(1) if L * D_MODEL * 4 >= (4 << 20) else None
    state_block = (sps, 1, 2, D_SSM, D_STATE)
    state_index = lambda b: (b, layer, 0, 0, 0)
    in_specs = [pl.BlockSpec((sps, L, D_MODEL), lambda b: (b, 0, 0), pipeline_mode=single),
                pl.BlockSpec((1, 1, 3, D_MODEL), (lambda b: (layer, 1 + b, 0, 0)) if mod_per_seq
                             else (lambda b: (layer, 0, 0, 0)))]
    args = [x, mod]
    if has_h0:
        in_specs.append(pl.BlockSpec(state_block, state_index))
        args.append(h0)
    stacked, shared = weights
    for w in stacked:
        in_specs.append(_param_spec(w, layer))
        args.append(w)
    for w in shared:
        in_specs.append(_param_spec(w, None))
        args.append(w)
    aliases = {}
    if state_aliased:
        aliases[len(args)] = 1
        in_specs.append(pl.BlockSpec(memory_space=pl.ANY))
        args.append(state_buf)
    out_shape = [jax.ShapeDtypeStruct((nb, L, D_MODEL), F32)]
    out_specs = [pl.BlockSpec((sps, L, D_MODEL), lambda b: (b, 0, 0), pipeline_mode=single)]
    if want_state:
        out_shape.append(jax.ShapeDtypeStruct((nb, DEPTH, 2, D_SSM, D_STATE), F32))
        out_specs.append(pl.BlockSpec(state_block, state_index))
    kernel = functools.partial(_layer_kernel, seqs_per_step=sps, n_inputs=len(args),
                               n_outputs=len(out_shape), seq_len=L, col_major=col_major,
                               has_h0=has_h0, want_state=want_state, final_norm=final_norm,
                               state_aliased=state_aliased)
    scratch = [
        pltpu.VMEM((L, D_MODEL), BF16),
        pltpu.VMEM((L if col_major else SUBLANES * 2, D_MODEL), BF16),
        pltpu.VMEM((2, L + 2 * CONV_PAD, CONV_TILE), F32),
        pltpu.VMEM((L, D_SSM), F32),
        pltpu.VMEM((L, N_GROUPS * D_STATE), F32),
        pltpu.VMEM((nc, N_GROUPS * D_STATE, CHUNK), F32),
        pltpu.VMEM((D_SSM // LANES, L, LANES), F32),
        pltpu.VMEM((2, D_STATE, D_SSM), F32),
        pltpu.VMEM((2, L, LANES), F32),
        pltpu.VMEM((ROW_TILE, D_GMLP), BF16),
        pltpu.VMEM((ROW_TILE, D_GMLP), F32),
        pltpu.VMEM((2, L, LANES), F32),
        pltpu.VMEM((2, nc, N_HEADS, CHUNK), F32),
        pltpu.VMEM((2, nc, N_HEADS, CHUNK), F32),
        pltpu.VMEM((2, nc, SUBLANES, D_SSM), F32),
    ]
    res = pl.pallas_call(
        kernel,
        grid=(nb // sps,),
        in_specs=in_specs,
        out_specs=out_specs,
        out_shape=out_shape,
        scratch_shapes=scratch,
        input_output_aliases=aliases,
        compiler_params=pltpu.CompilerParams(
            dimension_semantics=("arbitrary",), vmem_limit_bytes=VMEM_LIMIT_BYTES),
        name="layer_L%d%s" % (L, "_cm" if col_major else ""),
    )(*args)
    return res if want_state else (res[0], None)


def _prep_weights(w_in, norm_w, conv_w, conv_b, dt_bias, a_log, d_skip, ssm_norm_w, w_out_m,
                  sgu_ln_w, sgu_ln_b, w_sp, b_sp, w_out_g, w_out, final_norm_w):
    w_a = w_in[:, :, :DT_LO].astype(BF16)
    w_b = w_in[:, :, DT_HI:].astype(BF16)
    w_m, w_g, w_o = w_out_m.astype(BF16), w_out_g.astype(BF16), w_out.astype(BF16)
    w_dt = w_in[:, :, DT_LO:DT_HI].astype(BF16).reshape(DEPTH, D_MODEL, 2, N_HEADS)
    w_dt = jnp.pad(w_dt.transpose(0, 2, 1, 3), ((0, 0), (0, 0), (0, 0), (0, LANES - N_HEADS)))
    pad_h = ((0, 0), (0, 0), (0, LANES - N_HEADS))
    rows = lambda v: v.reshape(DEPTH, 1, -1)
    expand = (jnp.arange(LANES)[:, None] == (jnp.arange(D_SSM)[None, :] // HEAD_DIM)).astype(BF16)
    stacked = [
        w_a, w_b, w_dt, w_m, w_g, w_o, w_sp.astype(BF16),
        rows(norm_w), conv_w, rows(conv_b),
        jnp.pad(dt_bias, pad_h), jnp.pad(a_log, pad_h),
        rows(jnp.repeat(d_skip, HEAD_DIM, axis=-1)), rows(ssm_norm_w),
        rows(sgu_ln_w), rows(sgu_ln_b),
        jnp.repeat(b_sp.transpose(0, 2, 1), D_GMLP // N_GROUPS_GMLP, axis=-1),
    ]
    return stacked, [final_norm_w.reshape(1, -1), expand]


def kernel(x_prompt, x_sample, state_ssm, c, c_ctx, w_ada, b_ada, norm_w, w_in, conv_w, conv_b,
           dt_bias, a_log, d_skip, ssm_norm_w, w_out_m, sgu_ln_w, sgu_ln_b, w_sp, b_sp, w_out_g,
           w_out, final_norm_w):
    n_lat = c.shape[0]
    n_ctx = x_prompt.shape[0]
    cc = jnp.concatenate([c_ctx[None, :], c,
                          jnp.zeros((SUBLANES - 1 - n_lat, D_MODEL), F32)], axis=0)
    mod = _modulation(cc, w_ada, b_ada).reshape(DEPTH, SUBLANES, 3, D_MODEL)
    h0_lat = state_ssm.reshape(n_lat, DEPTH, 2, D_SSM, D_STATE)
    weights = _prep_weights(w_in, norm_w, conv_w, conv_b, dt_bias, a_log, d_skip, ssm_norm_w,
                            w_out_m, sgu_ln_w, sgu_ln_b, w_sp, b_sp, w_out_g, w_out, final_norm_w)

    h_ctx, h_lat = x_prompt, x_sample
    states = None
    for i in range(DEPTH):
        last = i == DEPTH - 1
        h_ctx, states = _layer_call(h_ctx, mod, None, weights, layer=i, state_buf=states,
                                    col_major=False, want_state=True, final_norm=last,
                                    mod_per_seq=False)
        h_lat, _ = _layer_call(h_lat, mod, h0_lat, weights, layer=i,
                               state_buf=None, col_major=(i % 2 == 1), want_state=False,
                               final_norm=last, mod_per_seq=True)
    return h_ctx, h_lat, states.reshape(n_ctx, DEPTH, 2, N_HEADS, HEAD_DIM, D_STATE)
```
